```python
import math
import jax, jax.numpy as jnp
from jax import lax
import numpy as np

D_MODEL = 1024
BATCH = 2
SEQ = 8192
DEPTH = 2

CHUNK = 64
PLE_DIM = 256
N_A = max(1, DEPTH // 2)
N_B = DEPTH - N_A
SSM_WIDTH = D_MODEL
GROUP_SIZE = 16
N_GROUPS = SSM_WIDTH // GROUP_SIZE
STATE = 64
DT_MIN = 0.001
DT_MAX = 0.1
HEAD_DIM = 64
SB_WIDTH = D_MODEL
N_HEADS = SB_WIDTH // HEAD_DIM
Q_BLOCK = 128
EPS = 1e-6

kernel_name = "hybrid_s5_stickbreaking_yoco"


def rms_norm(x, g):
    xf = x.astype(jnp.float32)
    y = xf * lax.rsqrt(jnp.mean(xf * xf, axis=-1, keepdims=True) + EPS)
    return (y * g.astype(jnp.float32)).astype(x.dtype)


def _linear_recurrence_combine(left, right):
    ar1, ai1, br1, bi1 = left
    ar2, ai2, br2, bi2 = right
    return (ar2 * ar1 - ai2 * ai1,
            ar2 * ai1 + ai2 * ar1,
            ar2 * br1 - ai2 * bi1 + br2,
            ar2 * bi1 + ai2 * br1 + bi2)


def s5_scan(u, lam_re, lam_im, log_dt, b_re, b_im, c_re, c_im):
    bsz, seq, _ = u.shape
    f32 = jnp.float32
    lr = jnp.minimum(lam_re.astype(f32), -1e-4)
    li = lam_im.astype(f32)
    dt = jnp.exp(log_dt.astype(f32))[:, None]
    mag = jnp.exp(lr * dt)
    a_re = mag * jnp.cos(li * dt)
    a_im = mag * jnp.sin(li * dt)
    den = lr * lr + li * li
    nr = a_re - 1.0
    f_re = (nr * lr + a_im * li) / den
    f_im = (a_im * lr - nr * li) / den
    br = b_re.astype(f32)
    bi = b_im.astype(f32)
    bb_re = f_re[..., None] * br - f_im[..., None] * bi
    bb_im = f_re[..., None] * bi + f_im[..., None] * br
    cr = c_re.astype(f32)
    ci = c_im.astype(f32)
    n_chunks = seq // CHUNK
    uc = u.astype(f32).reshape(bsz, n_chunks, CHUNK, N_GROUPS, GROUP_SIZE).transpose(1, 0, 2, 3, 4)

    def step(carry, u_chunk):
        h_re, h_im = carry
        x_re = jnp.einsum('bcgh,gph->bcgp', u_chunk, bb_re)
        x_im = jnp.einsum('bcgh,gph->bcgp', u_chunk, bb_im)
        x_re = x_re.at[:, 0].add(a_re * h_re - a_im * h_im)
        x_im = x_im.at[:, 0].add(a_re * h_im + a_im * h_re)
        ar = jnp.broadcast_to(a_re, x_re.shape)
        ai = jnp.broadcast_to(a_im, x_im.shape)
        _, _, s_re, s_im = lax.associative_scan(_linear_recurrence_combine, (ar, ai, x_re, x_im), axis=1)
        y = jnp.einsum('bcgp,ghp->bcgh', s_re, cr) - jnp.einsum('bcgp,ghp->bcgh', s_im, ci)
        return (s_re[:, -1], s_im[:, -1]), y

    h0 = jnp.zeros((bsz, N_GROUPS, STATE), f32)
    _, ys = lax.scan(step, (h0, h0), uc)
    return ys.transpose(1, 0, 2, 3, 4).reshape(bsz, seq, N_GROUPS * GROUP_SIZE)


def s5_mixer(h, w_in, lam_re, lam_im, log_dt, b_re, b_im, c_re, c_im, d_skip, w_glu, b_glu, w_out):
    f32 = jnp.float32
    proj = h @ w_in
    u, gate = jnp.split(proj, 2, axis=-1)
    y = s5_scan(u, lam_re, lam_im, log_dt, b_re, b_im, c_re, c_im) + d_skip.astype(f32) * u.astype(f32)
    g = jax.nn.gelu(y)
    y = g * jax.nn.sigmoid(g @ w_glu.astype(f32) + b_glu.astype(f32))
    y = y.astype(h.dtype) * jax.nn.silu(gate)
    return y @ w_out


def stick_breaking_attention(q, k, v):
    f32 = jnp.float32
    seq = q.shape[2]
    scale = HEAD_DIM ** -0.5
    outs = []
    for start in range(0, seq, Q_BLOCK):
        end = start + Q_BLOCK
        qb = q[:, :, start:end].astype(f32)
        kp = k[:, :, :end].astype(f32)
        vp = v[:, :, :end].astype(f32)
        z = jnp.einsum('bhqd,bhkd->bhqk', qb, kp) * scale
        mask = jnp.arange(end)[None, :] < jnp.arange(start, end)[:, None]
        log_keep = jnp.where(mask, jax.nn.log_sigmoid(-z), 0.0)
        tail = lax.cumsum(log_keep, axis=3, reverse=True) - log_keep
        w = jnp.where(mask, jnp.exp(jax.nn.log_sigmoid(z) + tail), 0.0)
        outs.append(jnp.einsum('bhqk,bhkd->bhqd', w, vp))
    return jnp.concatenate(outs, axis=2)


def split_heads(t):
    bsz, seq, _ = t.shape
    return t.reshape(bsz, seq, N_HEADS, HEAD_DIM).transpose(0, 2, 1, 3)


def shared_kv(x, kv_norm, w_kv):
    kv = rms_norm(x, kv_norm) @ w_kv
    k, v = jnp.split(kv, 2, axis=-1)
    return split_heads(k), split_heads(v)


def stick_breaking_mixer(h, k, v, w_in, w_out):
    bsz, seq, _ = h.shape
    proj = h @ w_in
    q, gate = jnp.split(proj, 2, axis=-1)
    o = stick_breaking_attention(split_heads(q), k, v)
    o = o.transpose(0, 2, 1, 3).reshape(bsz, seq, SB_WIDTH).astype(h.dtype)
    return (o * jax.nn.silu(gate)) @ w_out


def setup_inputs(seed: int = 0) -> dict:
    key = jax.random.key(seed)
    ks = jax.random.split(key, 24)
    f32 = jnp.float32
    nrm = lambda k, shape, s: jax.random.normal(k, shape, f32) * s
    gain = lambda k, shape: 1.0 + 0.05 * jax.random.normal(k, shape, f32)
    lam_im_base = jnp.pi * jnp.arange(STATE, dtype=f32)
    return {
        'x': jax.random.normal(ks[0], (BATCH, SEQ, D_MODEL), f32),
        'p': jax.random.normal(ks[1], (DEPTH, BATCH, SEQ, PLE_DIM), f32),
        'a_norm_pre': gain(ks[2], (N_A, D_MODEL)),
        'a_norm_post': gain(ks[3], (N_A, D_MODEL)),
        'a_w_in': nrm(ks[4], (N_A, D_MODEL, 2 * SSM_WIDTH), D_MODEL ** -0.5),
        'a_lam_re': -0.5 + nrm(ks[5], (N_A, N_GROUPS, STATE), 0.01),
        'a_lam_im': lam_im_base + nrm(ks[6], (N_A, N_GROUPS, STATE), 0.01),
        'a_log_dt': jax.random.uniform(ks[7], (N_A, N_GROUPS), f32, math.log(DT_MIN), math.log(DT_MAX)),
        'a_b_re': nrm(ks[8], (N_A, N_GROUPS, STATE, GROUP_SIZE), (2.0 * GROUP_SIZE) ** -0.5),
        'a_b_im': nrm(ks[9], (N_A, N_GROUPS, STATE, GROUP_SIZE), (2.0 * GROUP_SIZE) ** -0.5),
        'a_c_re': nrm(ks[10], (N_A, N_GROUPS, GROUP_SIZE, STATE), (2.0 * STATE) ** -0.5),
        'a_c_im': nrm(ks[11], (N_A, N_GROUPS, GROUP_SIZE, STATE), (2.0 * STATE) ** -0.5),
        'a_d_skip': nrm(ks[12], (N_A, SSM_WIDTH), 1.0),
        'a_w_glu': nrm(ks[13], (N_A, SSM_WIDTH, SSM_WIDTH), SSM_WIDTH ** -0.5),
        'a_b_glu': nrm(ks[14], (N_A, SSM_WIDTH), 0.01),
        'a_w_out': nrm(ks[15], (N_A, SSM_WIDTH, D_MODEL), SSM_WIDTH ** -0.5),
        'kv_norm': gain(ks[16], (D_MODEL,)),
        'w_kv': nrm(ks[17], (D_MODEL, 2 * SB_WIDTH), D_MODEL ** -0.5),
        'b_norm_pre': gain(ks[18], (N_B, D_MODEL)),
        'b_norm_post': gain(ks[19], (N_B, D_MODEL)),
        'b_w_in': nrm(ks[20], (N_B, D_MODEL, 2 * SB_WIDTH), D_MODEL ** -0.5),
        'b_w_out': nrm(ks[21], (N_B, SB_WIDTH, D_MODEL), SB_WIDTH ** -0.5),
        'ple_w_proj': nrm(ks[22], (DEPTH, PLE_DIM, D_MODEL), PLE_DIM ** -0.5),
        'ple_w_gate': nrm(ks[23], (DEPTH, D_MODEL, D_MODEL), D_MODEL ** -0.5),
    }


def reference(x, p, a_norm_pre, a_norm_post, a_w_in, a_lam_re, a_lam_im, a_log_dt, a_b_re, a_b_im,
              a_c_re, a_c_im, a_d_skip, a_w_glu, a_b_glu, a_w_out, kv_norm, w_kv,
              b_norm_pre, b_norm_post, b_w_in, b_w_out, ple_w_proj, ple_w_gate):
    k = None
    v = None
    for i in range(DEPTH):
        if i < N_A:
            j = i
            h = rms_norm(x, a_norm_pre[j])
            y = s5_mixer(h, a_w_in[j], a_lam_re[j], a_lam_im[j], a_log_dt[j], a_b_re[j], a_b_im[j],
                         a_c_re[j], a_c_im[j], a_d_skip[j], a_w_glu[j], a_b_glu[j], a_w_out[j])
            x = x + rms_norm(y, a_norm_post[j])
        else:
            j = i - N_A
            h = rms_norm(x, b_norm_pre[j])
            y = stick_breaking_mixer(h, k, v, b_w_in[j], b_w_out[j])
            x = x + rms_norm(y, b_norm_post[j])
        x = x + jax.nn.sigmoid(x @ ple_w_gate[i]) * (p[i] @ ple_w_proj[i])
        if i == N_A - 1:
            k, v = shared_kv(x, kv_norm, w_kv)
    return x
```

```python
import functools

import jax
import jax.numpy as jnp
from jax import lax
from jax.experimental import pallas as pl
from jax.experimental.pallas import tpu as pltpu

F32 = jnp.float32
BF16 = jnp.bfloat16

EPS = 1e-6
GROUP_SIZE = 16
STATE = 64
HEAD_DIM = 64
S5_CHUNK = 16
VMEM_LIMIT = 56 * 1024 * 1024


def _params(*sem):
    return pltpu.CompilerParams(dimension_semantics=sem, vmem_limit_bytes=VMEM_LIMIT)


def _rms(x, g):
    ms = jnp.mean(x * x, axis=-1, keepdims=True)
    return x * lax.rsqrt(ms + EPS) * g


def _bdot(a, b):
    return jnp.dot(a.astype(BF16), b, preferred_element_type=F32)


def _norm_matmul_kernel(x_ref, g_ref, w_ref, o_ref):
    h = _rms(x_ref[...], g_ref[...])
    o_ref[...] = _bdot(h, w_ref[...]).astype(o_ref.dtype)


def _norm_matmul(x, g, w, *, tm, name):
    m, d = x.shape
    n = w.shape[1]
    return pl.pallas_call(
        _norm_matmul_kernel,
        grid=(m // tm,),
        in_specs=[
            pl.BlockSpec((tm, d), lambda i: (i, 0)),
            pl.BlockSpec((1, d), lambda i: (0, 0)),
            pl.BlockSpec((d, n), lambda i: (0, 0)),
        ],
        out_specs=pl.BlockSpec((tm, n), lambda i: (i, 0)),
        out_shape=jax.ShapeDtypeStruct((m, n), BF16),
        compiler_params=_params("parallel"),
        name=name,
    )(x, g.reshape(1, d), w)


def _s5_conv_kernel(a_ref, w_ref, m_ref, v_ref, p1_ref, p2_ref, y_ref, *, levels):
    a = a_ref[...]
    s = jnp.dot(a, w_ref[...], preferred_element_type=F32)
    nc = s.shape[0]
    row = lax.broadcasted_iota(jnp.int32, s.shape, 0)
    h = s
    for k in range(levels):
        d = 1 << k
        sh = jnp.where(row >= d, pltpu.roll(h, d, axis=0), 0.0)
        h = h + p1_ref[k] * sh + p2_ref[k] * pltpu.roll(sh, STATE, axis=1)
    hprev = jnp.where(row >= 1, pltpu.roll(h, 1, axis=0), 0.0)
    y = jnp.dot(a, m_ref[...], preferred_element_type=F32)
    y = y + _bdot(hprev, v_ref[...])
    y_ref[...] = y


def _s5_conv(a, w, m, v, p1, p2):
    g, b, nc, k = a.shape
    levels = p1.shape[1]
    st2 = 2 * STATE
    return pl.pallas_call(
        functools.partial(_s5_conv_kernel, levels=levels),
        grid=(g, b),
        in_specs=[
            pl.BlockSpec((None, None, nc, k), lambda i, j: (i, j, 0, 0)),
            pl.BlockSpec((None, k, st2), lambda i, j: (i, 0, 0)),
            pl.BlockSpec((None, k, k), lambda i, j: (i, 0, 0)),
            pl.BlockSpec((None, st2, k), lambda i, j: (i, 0, 0)),
            pl.BlockSpec((None, levels, 1, st2), lambda i, j: (i, 0, 0, 0)),
            pl.BlockSpec((None, levels, 1, st2), lambda i, j: (i, 0, 0, 0)),
        ],
        out_specs=pl.BlockSpec((None, None, nc, k), lambda i, j: (i, j, 0, 0)),
        out_shape=jax.ShapeDtypeStruct((g, b, nc, k), F32),
        compiler_params=_params("parallel", "parallel"),
        name="s5_conv",
    )(a, w, m, v, p1, p2)


def _s5_weights(lam_re, lam_im, log_dt, b_re, b_im, c_re, c_im, nc):
    t = S5_CHUNK
    lr = jnp.minimum(lam_re.astype(F32), -1e-4)
    li = lam_im.astype(F32)
    dt = jnp.exp(log_dt.astype(F32))[:, None]
    mag = jnp.exp(lr * dt)
    a_re = mag * jnp.cos(li * dt)
    a_im = mag * jnp.sin(li * dt)
    den = lr * lr + li * li
    nr = a_re - 1.0
    f_re = (nr * lr + a_im * li) / den
    f_im = (a_im * lr - nr * li) / den
    br = b_re.astype(F32)
    bi = b_im.astype(F32)
    bb_re = f_re[..., None] * br - f_im[..., None] * bi
    bb_im = f_re[..., None] * bi + f_im[..., None] * br
    cr = c_re.astype(F32)
    ci = c_im.astype(F32)

    def cpow(n):
        n = n.astype(F32)[:, None, None]
        m = jnp.exp(lr * dt * n)
        th = li * dt * n
        return m * jnp.cos(th), m * jnp.sin(th)

    pr, pi = cpow(jnp.arange(t + 1))
    car = cr[None] * pr[:, :, None, :] - ci[None] * pi[:, :, None, :]
    cai = cr[None] * pi[:, :, None, :] + ci[None] * pr[:, :, None, :]
    kk = (jnp.einsum('nghp,gpk->nghk', car[:t], bb_re) - jnp.einsum('nghp,gpk->nghk', cai[:t], bb_im))
    lag = jnp.arange(t)[None, :] - jnp.arange(t)[:, None]
    kt = kk[jnp.clip(lag, 0, t - 1)]
    kt = jnp.where((lag >= 0)[:, :, None, None, None], kt, 0.0)
    m_mat = kt.transpose(2, 0, 4, 1, 3).reshape(-1, t * GROUP_SIZE, t * GROUP_SIZE)
    qr, qi = pr[:t][::-1], pi[:t][::-1]
    w_re = qr[..., None] * bb_re[None] - qi[..., None] * bb_im[None]
    w_im = qr[..., None] * bb_im[None] + qi[..., None] * bb_re[None]
    w_mat = jnp.concatenate([w_re, w_im], axis=2).transpose(1, 0, 3, 2).reshape(-1, t * GROUP_SIZE, 2 * STATE)
    v_re = car[1:].transpose(1, 3, 0, 2)
    v_im = -cai[1:].transpose(1, 3, 0, 2)
    v_mat = jnp.concatenate([v_re, v_im], axis=1).reshape(-1, 2 * STATE, t * GROUP_SIZE)
    levels = max(1, (nc - 1).bit_length())
    sr, si = cpow(t * (2 ** jnp.arange(levels)))
    p1 = jnp.concatenate([sr, sr], axis=-1).transpose(1, 0, 2)[:, :, None, :]
    p2 = jnp.concatenate([-si, si], axis=-1).transpose(1, 0, 2)[:, :, None, :]
    return w_mat.astype(BF16), m_mat.astype(BF16), v_mat.astype(BF16), p1, p2


def _tail(x, m, gpost, p, wpg, wpp):
    x1 = x + _rms(m, gpost)
    pg = _bdot(x1, wpg)
    pp = _bdot(p, wpp)
    return x1 + jax.nn.sigmoid(pg) * pp


def _s5_post_kernel(x_ref, ys_ref, u_ref, gate_ref, p_ref, dskip_ref, wglu_ref, bglu_ref, wout_ref,
                    gpost_ref, wpg_ref, wpp_ref, o_ref):
    u = u_ref[...].astype(F32)
    y = ys_ref[...] + dskip_ref[...] * u
    g = jax.nn.gelu(y)
    z = _bdot(g, wglu_ref[...]) + bglu_ref[...]
    y = g * jax.nn.sigmoid(z)
    gate = gate_ref[...].astype(F32)
    y = y * (gate * jax.nn.sigmoid(gate))
    m = _bdot(y, wout_ref[...])
    o_ref[...] = _tail(x_ref[...], m, gpost_ref[...], p_ref[...], wpg_ref[...], wpp_ref[...])


def _s5_post(x, ys, proj, p, dskip, wglu, bglu, wout, gpost, wpg, wpp, *, tm):
    m, d = x.shape
    pd = p.shape[1]
    row = lambda i: (i, 0)
    const = lambda i: (0, 0)
    return pl.pallas_call(
        _s5_post_kernel,
        grid=(m // tm,),
        in_specs=[
            pl.BlockSpec((tm, d), row),
            pl.BlockSpec((tm, d), row),
            pl.BlockSpec((tm, d), lambda i: (i, 0)),
            pl.BlockSpec((tm, d), lambda i: (i, 1)),
            pl.BlockSpec((tm, pd), row),
            pl.BlockSpec((1, d), const),
            pl.BlockSpec((d, d), const),
            pl.BlockSpec((1, d), const),
            pl.BlockSpec((d, d), const),
            pl.BlockSpec((1, d), const),
            pl.BlockSpec((d, d), const),
            pl.BlockSpec((pd, d), const),
        ],
        out_specs=pl.BlockSpec((tm, d), row),
        out_shape=jax.ShapeDtypeStruct((m, d), F32),
        compiler_params=_params("parallel"),
        name="s5_post",
    )(x, ys, proj, proj, p, dskip.reshape(1, d), wglu, bglu.reshape(1, d), wout, gpost.reshape(1, d), wpg, wpp)


def _attn_post_kernel(x_ref, og_ref, p_ref, wout_ref, gpost_ref, wpg_ref, wpp_ref, o_ref):
    m = jnp.dot(og_ref[...], wout_ref[...], preferred_element_type=F32)
    o_ref[...] = _tail(x_ref[...], m, gpost_ref[...], p_ref[...], wpg_ref[...], wpp_ref[...])


def _attn_post(x, og, p, wout, gpost, wpg, wpp, *, tm):
    m, d = x.shape
    pd = p.shape[1]
    row = lambda i: (i, 0)
    const = lambda i: (0, 0)
    return pl.pallas_call(
        _attn_post_kernel,
        grid=(m // tm,),
        in_specs=[
            pl.BlockSpec((tm, d), row),
            pl.BlockSpec((tm, d), row),
            pl.BlockSpec((tm, pd), row),
            pl.BlockSpec((d, d), const),
            pl.BlockSpec((1, d), const),
            pl.BlockSpec((d, d), const),
            pl.BlockSpec((pd, d), const),
        ],
        out_specs=pl.BlockSpec((tm, d), row),
        out_shape=jax.ShapeDtypeStruct((m, d), F32),
        compiler_params=_params("parallel"),
        name="attn_post",
    )(x, og, p, wout, gpost.reshape(1, d), wpg, wpp)


def _sb_attn_kernel(q_ref, gate_ref, k_ref, v_ref, o_ref, acc_ref, carry_ref, *, tq):
    i = pl.program_id(2)
    lane = lax.broadcasted_iota(jnp.int32, (tq, 2 * HEAD_DIM), 1)
    r = lax.broadcasted_iota(jnp.int32, (tq, tq), 0)
    c = lax.broadcasted_iota(jnp.int32, (tq, tq), 1)
    upper = jnp.where(r > c, -1.0, 0.0).astype(BF16)
    ones = jnp.full((tq, 2 * HEAD_DIM), -1.0, BF16)
    causal = c < r
    q = q_ref[...] * jnp.asarray(HEAD_DIM ** -0.5, BF16)

    def tile(qh, kj, vj, carry, mask):
        z = lax.dot_general(qh, kj, (((1,), (1,)), ((), ())), preferred_element_type=F32)
        sp = jnp.maximum(z, 0.0) + jnp.log(1.0 + jnp.exp(-jnp.abs(z)))
        if mask is not None:
            sp = jnp.where(mask, sp, 0.0)
        spb = sp.astype(BF16)
        tail = jnp.dot(spb, upper, preferred_element_type=F32)
        rs = jnp.dot(spb, ones, preferred_element_type=F32)
        logw = (z - sp) + tail + jnp.concatenate([carry] * (tq // (2 * HEAD_DIM)), axis=1)
        w = jnp.exp(logw)
        if mask is not None:
            w = jnp.where(mask, w, 0.0)
        pv = jnp.dot(w.astype(BF16), vj, preferred_element_type=F32)
        return pv, carry + rs

    outs = []
    for h in range(2):
        qh = jnp.where((lane // HEAD_DIM) == h, q, jnp.zeros_like(q))
        start = pl.multiple_of(i * tq, tq)
        pv, carry = tile(qh, k_ref[pl.ds(start, tq), :], v_ref[pl.ds(start, tq), :],
                         jnp.zeros((tq, 2 * HEAD_DIM), F32), causal)
        acc_ref[...] = pv
        carry_ref[...] = carry

        def body(n, _):
            j = i - 1 - n
            st = pl.multiple_of(j * tq, tq)
            pv, carry = tile(qh, k_ref[pl.ds(st, tq), :], v_ref[pl.ds(st, tq), :], carry_ref[...], None)
            acc_ref[...] += pv
            carry_ref[...] = carry
            return 0

        lax.fori_loop(0, i, body, 0)
        outs.append(acc_ref[...])
    o = jnp.where(lane < HEAD_DIM, outs[0], outs[1])
    gate = gate_ref[...].astype(F32)
    o_ref[...] = (o * (gate * jax.nn.sigmoid(gate))).astype(o_ref.dtype)


def _sb_attn(qg, kv, *, tq):
    b, l, d2 = qg.shape
    d = d2 // 2
    lanes = 2 * HEAD_DIM
    nhp = d // lanes
    return pl.pallas_call(
        functools.partial(_sb_attn_kernel, tq=tq),
        grid=(b, nhp, l // tq),
        in_specs=[
            pl.BlockSpec((None, tq, lanes), lambda bi, hp, i: (bi, i, hp)),
            pl.BlockSpec((None, tq, lanes), lambda bi, hp, i: (bi, i, nhp + hp)),
            pl.BlockSpec((None, l, lanes), lambda bi, hp, i: (bi, 0, hp)),
            pl.BlockSpec((None, l, lanes), lambda bi, hp, i: (bi, 0, nhp + hp)),
        ],
        out_specs=pl.BlockSpec((None, tq, lanes), lambda bi, hp, i: (bi, i, hp)),
        out_shape=jax.ShapeDtypeStruct((b, l, d), BF16),
        scratch_shapes=[pltpu.VMEM((tq, lanes), F32), pltpu.VMEM((tq, lanes), F32)],
        compiler_params=_params("parallel", "parallel", "arbitrary"),
        name="sb_attn",
    )(qg, qg, kv, kv)


def kernel(x, p, a_norm_pre, a_norm_post, a_w_in, a_lam_re, a_lam_im, a_log_dt, a_b_re, a_b_im, a_c_re, a_c_im, a_d_skip, a_w_glu, a_b_glu, a_w_out, kv_norm, w_kv, b_norm_pre, b_norm_post, b_w_in, b_w_out, ple_w_proj, ple_w_gate):
    bsz, seq, d = x.shape
    depth = p.shape[0]
    n_a = a_w_in.shape[0]
    tokens = bsz * seq
    tm = 256
    nc = seq // S5_CHUNK
    n_groups = d // GROUP_SIZE
    xf = x.reshape(tokens, d)
    k_v = None
    for i in range(depth):
        pi = p[i].reshape(tokens, -1)
        wpg = ple_w_gate[i].astype(BF16)
        wpp = ple_w_proj[i].astype(BF16)
        if i < n_a:
            j = i
            proj = _norm_matmul(xf, a_norm_pre[j], a_w_in[j].astype(BF16), tm=tm, name="s5_in_proj")
            u = proj[:, :d].reshape(bsz, nc, S5_CHUNK, n_groups, GROUP_SIZE)
            a = u.transpose(3, 0, 1, 2, 4).reshape(n_groups, bsz, nc, S5_CHUNK * GROUP_SIZE)
            wm, mm, vm, p1, p2 = _s5_weights(a_lam_re[j], a_lam_im[j], a_log_dt[j], a_b_re[j], a_b_im[j],
                                             a_c_re[j], a_c_im[j], nc)
            yc = _s5_conv(a, wm, mm, vm, p1, p2)
            ys = yc.reshape(n_groups, bsz, nc, S5_CHUNK, GROUP_SIZE).transpose(1, 2, 3, 0, 4).reshape(tokens, d)
            xf = _s5_post(xf, ys, proj, pi, a_d_skip[j], a_w_glu[j].astype(BF16), a_b_glu[j],
                          a_w_out[j].astype(BF16), a_norm_post[j], wpg, wpp, tm=tm)
        else:
            j = i - n_a
            qg = _norm_matmul(xf, b_norm_pre[j], b_w_in[j].astype(BF16), tm=tm, name="attn_in_proj")
            og = _sb_attn(qg.reshape(bsz, seq, 2 * d), k_v, tq=256)
            xf = _attn_post(xf, og.reshape(tokens, d), pi, b_w_out[j].astype(BF16), b_norm_post[j], wpg, wpp, tm=tm)
        if i == n_a - 1:
            k_v = _norm_matmul(xf, kv_norm, w_kv.astype(BF16), tm=tm, name="kv_proj").reshape(bsz, seq, 2 * d)
    return xf.reshape(bsz, seq, d)
```

```python
import functools

import jax
import jax.numpy as jnp
from jax import lax
from jax.experimental import pallas as pl
from jax.experimental.pallas import tpu as pltpu

F32 = jnp.float32
BF16 = jnp.bfloat16

EPS = 1e-6
GROUP_SIZE = 16
STATE = 64
HEAD_DIM = 64
S5_CHUNK = 16
VMEM_LIMIT = 56 * 1024 * 1024


def _params(*sem):
    return pltpu.CompilerParams(dimension_semantics=sem, vmem_limit_bytes=VMEM_LIMIT)


def _rms(x, g):
    ms = jnp.mean(x * x, axis=-1, keepdims=True)
    return x * lax.rsqrt(ms + EPS) * g


def _bdot(a, b):
    return jnp.dot(a.astype(BF16), b, preferred_element_type=F32)


def _norm_matmul_kernel(x_ref, g_ref, w_ref, o_ref):
    h = _rms(x_ref[...], g_ref[...])
    o_ref[...] = _bdot(h, w_ref[...]).astype(o_ref.dtype)


def _norm_matmul(x, g, w, *, tm, name):
    m, d = x.shape
    n = w.shape[1]
    return pl.pallas_call(
        _norm_matmul_kernel,
        grid=(m // tm,),
        in_specs=[
            pl.BlockSpec((tm, d), lambda i: (i, 0)),
            pl.BlockSpec((1, d), lambda i: (0, 0)),
            pl.BlockSpec((d, n), lambda i: (0, 0)),
        ],
        out_specs=pl.BlockSpec((tm, n), lambda i: (i, 0)),
        out_shape=jax.ShapeDtypeStruct((m, n), BF16),
        compiler_params=_params("parallel"),
        name=name,
    )(x, g.reshape(1, d), w)


def _s5_conv_kernel(a_ref, w_ref, m_ref, v_ref, p1_ref, p2_ref, y_ref, *, levels):
    a = a_ref[...]
    s = jnp.dot(a, w_ref[...], preferred_element_type=F32)
    nc = s.shape[0]
    row = lax.broadcasted_iota(jnp.int32, s.shape, 0)
    h = s
    for k in range(levels):
        d = 1 << k
        sh = jnp.where(row >= d, pltpu.roll(h, d, axis=0), 0.0)
        h = h + p1_ref[k] * sh + p2_ref[k] * pltpu.roll(sh, STATE, axis=1)
    hprev = jnp.where(row >= 1, pltpu.roll(h, 1, axis=0), 0.0)
    y = jnp.dot(a, m_ref[...], preferred_element_type=F32)
    y = y + _bdot(hprev, v_ref[...])
    y_ref[...] = y


def _s5_conv(a, w, m, v, p1, p2):
    g, b, nc, k = a.shape
    levels = p1.shape[1]
    st2 = 2 * STATE
    return pl.pallas_call(
        functools.partial(_s5_conv_kernel, levels=levels),
        grid=(g, b),
        in_specs=[
            pl.BlockSpec((None, None, nc, k), lambda i, j: (i, j, 0, 0)),
            pl.BlockSpec((None, k, st2), lambda i, j: (i, 0, 0)),
            pl.BlockSpec((None, k, k), lambda i, j: (i, 0, 0)),
            pl.BlockSpec((None, st2, k), lambda i, j: (i, 0, 0)),
            pl.BlockSpec((None, levels, 1, st2), lambda i, j: (i, 0, 0, 0)),
            pl.BlockSpec((None, levels, 1, st2), lambda i, j: (i, 0, 0, 0)),
        ],
        out_specs=pl.BlockSpec((None, None, nc, k), lambda i, j: (i, j, 0, 0)),
        out_shape=jax.ShapeDtypeStruct((g, b, nc, k), F32),
        compiler_params=_params("parallel", "parallel"),
        name="s5_conv",
    )(a, w, m, v, p1, p2)


def _s5_weights(lam_re, lam_im, log_dt, b_re, b_im, c_re, c_im, nc):
    t = S5_CHUNK
    lr = jnp.minimum(lam_re.astype(F32), -1e-4)
    li = lam_im.astype(F32)
    dt = jnp.exp(log_dt.astype(F32))[:, None]
    mag = jnp.exp(lr * dt)
    a_re = mag * jnp.cos(li * dt)
    a_im = mag * jnp.sin(li * dt)
    den = lr * lr + li * li
    nr = a_re - 1.0
    f_re = (nr * lr + a_im * li) / den
    f_im = (a_im * lr - nr * li) / den
    br = b_re.astype(F32)
    bi = b_im.astype(F32)
    bb_re = f_re[..., None] * br - f_im[..., None] * bi
    bb_im = f_re[..., None] * bi + f_im[..., None] * br
    cr = c_re.astype(F32)
    ci = c_im.astype(F32)

    def cpow(n):
        n = n.astype(F32)[:, None, None]
        m = jnp.exp(lr * dt * n)
        th = li * dt * n
        return m * jnp.cos(th), m * jnp.sin(th)

    pr, pi = cpow(jnp.arange(t + 1))
    car = cr[None] * pr[:, :, None, :] - ci[None] * pi[:, :, None, :]
    cai = cr[None] * pi[:, :, None, :] + ci[None] * pr[:, :, None, :]
    kk = (jnp.einsum('nghp,gpk->nghk', car[:t], bb_re) - jnp.einsum('nghp,gpk->nghk', cai[:t], bb_im))
    lag = jnp.arange(t)[None, :] - jnp.arange(t)[:, None]
    kt = kk[jnp.clip(lag, 0, t - 1)]
    kt = jnp.where((lag >= 0)[:, :, None, None, None], kt, 0.0)
    m_mat = kt.transpose(2, 0, 4, 1, 3).reshape(-1, t * GROUP_SIZE, t * GROUP_SIZE)
    qr, qi = pr[:t][::-1], pi[:t][::-1]
    w_re = qr[..., None] * bb_re[None] - qi[..., None] * bb_im[None]
    w_im = qr[..., None] * bb_im[None] + qi[..., None] * bb_re[None]
    w_mat = jnp.concatenate([w_re, w_im], axis=2).transpose(1, 0, 3, 2).reshape(-1, t * GROUP_SIZE, 2 * STATE)
    v_re = car[1:].transpose(1, 3, 0, 2)
    v_im = -cai[1:].transpose(1, 3, 0, 2)
    v_mat = jnp.concatenate([v_re, v_im], axis=1).reshape(-1, 2 * STATE, t * GROUP_SIZE)
    levels = max(1, (nc - 1).bit_length())
    sr, si = cpow(t * (2 ** jnp.arange(levels)))
    p1 = jnp.concatenate([sr, sr], axis=-1).transpose(1, 0, 2)[:, :, None, :]
    p2 = jnp.concatenate([-si, si], axis=-1).transpose(1, 0, 2)[:, :, None, :]
    return w_mat.astype(BF16), m_mat.astype(BF16), v_mat.astype(BF16), p1, p2


def _tail(x, m, gpost, p, wpg, wpp):
    x1 = x + _rms(m, gpost)
    pg = _bdot(x1, wpg)
    pp = _bdot(p, wpp)
    return x1 + jax.nn.sigmoid(pg) * pp


def _s5_post_kernel(x_ref, ys_ref, u_ref, gate_ref, p_ref, dskip_ref, wglu_ref, bglu_ref, wout_ref,
                    gpost_ref, wpg_ref, wpp_ref, o_ref):
    u = u_ref[...].astype(F32)
    y = ys_ref[...] + dskip_ref[...] * u
    g = jax.nn.gelu(y)
    z = _bdot(g, wglu_ref[...]) + bglu_ref[...]
    y = g * jax.nn.sigmoid(z)
    gate = gate_ref[...].astype(F32)
    y = y * (gate * jax.nn.sigmoid(gate))
    m = _bdot(y, wout_ref[...])
    o_ref[...] = _tail(x_ref[...], m, gpost_ref[...], p_ref[...], wpg_ref[...], wpp_ref[...])


def _s5_post(x, ys, proj, p, dskip, wglu, bglu, wout, gpost, wpg, wpp, *, tm):
    m, d = x.shape
    pd = p.shape[1]
    row = lambda i: (i, 0)
    const = lambda i: (0, 0)
    return pl.pallas_call(
        _s5_post_kernel,
        grid=(m // tm,),
        in_specs=[
            pl.BlockSpec((tm, d), row),
            pl.BlockSpec((tm, d), row),
            pl.BlockSpec((tm, d), lambda i: (i, 0)),
            pl.BlockSpec((tm, d), lambda i: (i, 1)),
            pl.BlockSpec((tm, pd), row),
            pl.BlockSpec((1, d), const),
            pl.BlockSpec((d, d), const),
            pl.BlockSpec((1, d), const),
            pl.BlockSpec((d, d), const),
            pl.BlockSpec((1, d), const),
            pl.BlockSpec((d, d), const),
            pl.BlockSpec((pd, d), const),
        ],
        out_specs=pl.BlockSpec((tm, d), row),
        out_shape=jax.ShapeDtypeStruct((m, d), F32),
        compiler_params=_params("parallel"),
        name="s5_post",
    )(x, ys, proj, proj, p, dskip.reshape(1, d), wglu, bglu.reshape(1, d), wout, gpost.reshape(1, d), wpg, wpp)


def _attn_post_kernel(x_ref, og_ref, p_ref, wout_ref, gpost_ref, wpg_ref, wpp_ref, o_ref):
    m = jnp.dot(og_ref[...], wout_ref[...], preferred_element_type=F32)
    o_ref[...] = _tail(x_ref[...], m, gpost_ref[...], p_ref[...], wpg_ref[...], wpp_ref[...])


def _attn_post(x, og, p, wout, gpost, wpg, wpp, *, tm):
    m, d = x.shape
    pd = p.shape[1]
    row = lambda i: (i, 0)
    const = lambda i: (0, 0)
    return pl.pallas_call(
        _attn_post_kernel,
        grid=(m // tm,),
        in_specs=[
            pl.BlockSpec((tm, d), row),
            pl.BlockSpec((tm, d), row),
            pl.BlockSpec((tm, pd), row),
            pl.BlockSpec((d, d), const),
            pl.BlockSpec((1, d), const),
            pl.BlockSpec((d, d), const),
            pl.BlockSpec((pd, d), const),
        ],
        out_specs=pl.BlockSpec((tm, d), row),
        out_shape=jax.ShapeDtypeStruct((m, d), F32),
        compiler_params=_params("parallel"),
        name="attn_post",
    )(x, og, p, wout, gpost.reshape(1, d), wpg, wpp)


ATTN_HEADS = 4
ATTN_TQ = 128
ATTN_TQ_OUTER = 1024
LOG_F32_UNDERFLOW = 104.0


def _sb_attn_kernel(q_ref, gate_ref, k_ref, v_ref, o_ref, acc_ref, c_ref):
    tq, nh = ATTN_TQ, ATTN_HEADS
    lanes = nh * HEAD_DIM
    rows = nh * tq
    n_inner = q_ref.shape[0] // tq
    io = pl.program_id(2)
    lane_head = lax.broadcasted_iota(jnp.int32, (tq, lanes), 1) // HEAD_DIM
    kr = lax.broadcasted_iota(jnp.int32, (tq, 2 * tq), 0)
    kc = lax.broadcasted_iota(jnp.int32, (tq, 2 * tq), 1)
    tri_ones = jnp.where((kr > kc) | (kc >= tq), 1.0, 0.0).astype(BF16)
    qrow = lax.broadcasted_iota(jnp.int32, (rows, tq), 0) % tq
    kcol = lax.broadcasted_iota(jnp.int32, (rows, tq), 1)
    causal = kcol < qrow
    scale = jnp.asarray(HEAD_DIM ** -0.5, BF16)

    def tile(q4, j, c_prev, mask):
        st = pl.multiple_of(j * tq, tq)
        kj = k_ref[pl.ds(st, tq), :]
        vj = v_ref[pl.ds(st, tq), :]
        z = lax.dot_general(q4, kj, (((1,), (1,)), ((), ())), preferred_element_type=F32)
        sp = jnp.maximum(z, 0.0) + jnp.log(1.0 + jnp.exp(-jnp.abs(z)))
        if mask is not None:
            sp = jnp.where(mask, sp, 0.0)
        both = jnp.dot(sp.astype(BF16), tri_ones, preferred_element_type=F32)
        tail, rs = both[:, :tq], both[:, tq:]
        logw = (z - sp) - tail
        if c_prev is not None:
            logw = logw - c_prev
        w = jnp.exp(logw)
        if mask is not None:
            w = jnp.where(mask, w, 0.0)
        pv = jnp.dot(w.astype(BF16), vj, preferred_element_type=F32)
        return pv, rs

    def q_block(qi, _):
        r0 = pl.multiple_of(qi * tq, tq)
        i = io * n_inner + qi
        q = q_ref[pl.ds(r0, tq), :] * scale
        q4 = jnp.concatenate([jnp.where(lane_head == h, q, jnp.zeros_like(q)) for h in range(nh)], axis=0)
        pv, rs = tile(q4, i, None, causal)
        acc_ref[...] = pv
        c_ref[...] = rs

        def cond(state):
            j, cmin = state
            return (j >= 0) & (cmin < LOG_F32_UNDERFLOW)

        def body(state):
            j, _ = state
            c_prev = c_ref[...]
            pv, rs = tile(q4, j, c_prev, None)
            acc_ref[...] += pv
            c_new = c_prev + rs
            c_ref[...] = c_new
            return j - 1, jnp.min(c_new)

        lax.while_loop(cond, body, (i - 1, jnp.min(rs)))
        o = jnp.zeros((tq, lanes), F32)
        for h in range(nh):
            o = jnp.where(lane_head == h, acc_ref[h * tq:(h + 1) * tq, :], o)
        gate = gate_ref[pl.ds(r0, tq), :].astype(F32)
        o_ref[pl.ds(r0, tq), :] = (o * (gate * jax.nn.sigmoid(gate))).astype(o_ref.dtype)
        return 0

    lax.fori_loop(0, n_inner, q_block, 0)


def _sb_attn(qg, kv):
    b, l, d2 = qg.shape
    d = d2 // 2
    lanes = ATTN_HEADS * HEAD_DIM
    nhg = d // lanes
    tqo = min(ATTN_TQ_OUTER, l)
    return pl.pallas_call(
        _sb_attn_kernel,
        grid=(b, nhg, l // tqo),
        in_specs=[
            pl.BlockSpec((None, tqo, lanes), lambda bi, hg, i: (bi, i, hg)),
            pl.BlockSpec((None, tqo, lanes), lambda bi, hg, i: (bi, i, nhg + hg)),
            pl.BlockSpec((None, l, lanes), lambda bi, hg, i: (bi, 0, hg)),
            pl.BlockSpec((None, l, lanes), lambda bi, hg, i: (bi, 0, nhg + hg)),
        ],
        out_specs=pl.BlockSpec((None, tqo, lanes), lambda bi, hg, i: (bi, i, hg)),
        out_shape=jax.ShapeDtypeStruct((b, l, d), BF16),
        scratch_shapes=[pltpu.VMEM((ATTN_HEADS * ATTN_TQ, lanes), F32),
                        pltpu.VMEM((ATTN_HEADS * ATTN_TQ, ATTN_TQ), F32)],
        compiler_params=_params("parallel", "parallel", "arbitrary"),
        name="sb_attn",
    )(qg, qg, kv, kv)


def kernel(x, p, a_norm_pre, a_norm_post, a_w_in, a_lam_re, a_lam_im, a_log_dt, a_b_re, a_b_im, a_c_re, a_c_im, a_d_skip, a_w_glu, a_b_glu, a_w_out, kv_norm, w_kv, b_norm_pre, b_norm_post, b_w_in, b_w_out, ple_w_proj, ple_w_gate):
    bsz, seq, d = x.shape
    depth = p.shape[0]
    n_a = a_w_in.shape[0]
    tokens = bsz * seq
    tm = 256
    nc = seq // S5_CHUNK
    n_groups = d // GROUP_SIZE
    xf = x.reshape(tokens, d)
    k_v = None
    for i in range(depth):
        pi = p[i].reshape(tokens, -1)
        wpg = ple_w_gate[i].astype(BF16)
        wpp = ple_w_proj[i].astype(BF16)
        if i < n_a:
            j = i
            proj = _norm_matmul(xf, a_norm_pre[j], a_w_in[j].astype(BF16), tm=tm, name="s5_in_proj")
            u = proj[:, :d].reshape(bsz, nc, S5_CHUNK, n_groups, GROUP_SIZE)
            a = u.transpose(3, 0, 1, 2, 4).reshape(n_groups, bsz, nc, S5_CHUNK * GROUP_SIZE)
            wm, mm, vm, p1, p2 = _s5_weights(a_lam_re[j], a_lam_im[j], a_log_dt[j], a_b_re[j], a_b_im[j],
                                             a_c_re[j], a_c_im[j], nc)
            yc = _s5_conv(a, wm, mm, vm, p1, p2)
            ys = yc.reshape(n_groups, bsz, nc, S5_CHUNK, GROUP_SIZE).transpose(1, 2, 3, 0, 4).reshape(tokens, d)
            xf = _s5_post(xf, ys, proj, pi, a_d_skip[j], a_w_glu[j].astype(BF16), a_b_glu[j],
                          a_w_out[j].astype(BF16), a_norm_post[j], wpg, wpp, tm=tm)
        else:
            j = i - n_a
            qg = _norm_matmul(xf, b_norm_pre[j], b_w_in[j].astype(BF16), tm=tm, name="attn_in_proj")
            og = _sb_attn(qg.reshape(bsz, seq, 2 * d), k_v)
            xf = _attn_post(xf, og.reshape(tokens, d), pi, b_w_out[j].astype(BF16), b_norm_post[j], wpg, wpp, tm=tm)
        if i == n_a - 1:
            k_v = _norm_matmul(xf, kv_norm, w_kv.astype(BF16), tm=tm, name="kv_proj").reshape(bsz, seq, 2 * d)
    return xf.reshape(bsz, seq, d)
```

```python
import functools

import jax
import jax.numpy as jnp
from jax import lax
from jax.experimental import pallas as pl
from jax.experimental.pallas import tpu as pltpu

F32 = jnp.float32
BF16 = jnp.bfloat16

EPS = 1e-6
GROUP_SIZE = 16
STATE = 64
HEAD_DIM = 64
S5_CHUNK = 16
VMEM_LIMIT = 56 * 1024 * 1024


def _params(*sem):
    return pltpu.CompilerParams(dimension_semantics=sem, vmem_limit_bytes=VMEM_LIMIT)


def _rms(x, g):
    ms = jnp.mean(x * x, axis=-1, keepdims=True)
    return x * lax.rsqrt(ms + EPS) * g


def _bdot(a, b):
    return jnp.dot(a.astype(BF16), b, preferred_element_type=F32)


def _norm_matmul_kernel(x_ref, g_ref, w_ref, o_ref):
    h = _rms(x_ref[...], g_ref[...])
    o_ref[...] = _bdot(h, w_ref[...]).astype(o_ref.dtype)


def _norm_matmul(x, g, w, *, tm, name):
    m, d = x.shape
    n = w.shape[1]
    return pl.pallas_call(
        _norm_matmul_kernel,
        grid=(m // tm,),
        in_specs=[
            pl.BlockSpec((tm, d), lambda i: (i, 0)),
            pl.BlockSpec((1, d), lambda i: (0, 0)),
            pl.BlockSpec((d, n), lambda i: (0, 0)),
        ],
        out_specs=pl.BlockSpec((tm, n), lambda i: (i, 0)),
        out_shape=jax.ShapeDtypeStruct((m, n), BF16),
        compiler_params=_params("parallel"),
        name=name,
    )(x, g.reshape(1, d), w)


def _s5_in_proj_kernel(x_ref, g_ref, wut_ref, wg_ref, ut_ref, gate_ref):
    hn = _rms(x_ref[...], g_ref[...]).astype(BF16)
    ut = lax.dot_general(wut_ref[...], hn, (((1,), (1,)), ((), ())), preferred_element_type=F32)
    n_groups = ut_ref.shape[0]
    ut = ut.astype(BF16)
    for g in range(n_groups):
        ut_ref[g] = ut[g * GROUP_SIZE:(g + 1) * GROUP_SIZE, :]
    gate_ref[...] = jnp.dot(hn, wg_ref[...], preferred_element_type=F32).astype(gate_ref.dtype)


def _s5_in_proj(x3, g, wut, wg):
    bsz, nc, td = x3.shape
    d = td // S5_CHUNK
    n_groups = d // GROUP_SIZE
    return pl.pallas_call(
        _s5_in_proj_kernel,
        grid=(bsz, S5_CHUNK),
        in_specs=[
            pl.BlockSpec((None, nc, d), lambda b, t: (b, 0, t)),
            pl.BlockSpec((1, d), lambda b, t: (0, 0)),
            pl.BlockSpec((d, d), lambda b, t: (0, 0)),
            pl.BlockSpec((d, d), lambda b, t: (0, 0)),
        ],
        out_specs=[
            pl.BlockSpec((n_groups, GROUP_SIZE, nc), lambda b, t: (0, t, b)),
            pl.BlockSpec((nc, d), lambda b, t: (b * S5_CHUNK + t, 0)),
        ],
        out_shape=[
            jax.ShapeDtypeStruct((n_groups, S5_CHUNK * GROUP_SIZE, bsz * nc), BF16),
            jax.ShapeDtypeStruct((bsz * S5_CHUNK * nc, d), BF16),
        ],
        compiler_params=_params("parallel", "parallel"),
        name="s5_in_proj",
    )(x3, g.reshape(1, d), wut, wg)


def _s5_conv_kernel(at_ref, wt_ref, mt_ref, vt_ref, pr_ref, pi_ref, yt_ref, *, nc, levels):
    at = at_ref[...]
    s = jnp.dot(wt_ref[...], at, preferred_element_type=F32)
    seg = lax.broadcasted_iota(jnp.int32, (STATE, s.shape[1]), 1) % nc
    hr, hi = s[:STATE], s[STATE:]
    for k in range(levels):
        d = 1 << k
        keep = seg >= d
        sr = jnp.where(keep, pltpu.roll(hr, d, axis=1), 0.0)
        si = jnp.where(keep, pltpu.roll(hi, d, axis=1), 0.0)
        pr, pi = pr_ref[k], pi_ref[k]
        hr, hi = hr + pr * sr - pi * si, hi + pr * si + pi * sr
    keep = seg >= 1
    hprev = jnp.concatenate([jnp.where(keep, pltpu.roll(hr, 1, axis=1), 0.0),
                             jnp.where(keep, pltpu.roll(hi, 1, axis=1), 0.0)], axis=0)
    y = jnp.dot(mt_ref[...], at, preferred_element_type=F32)
    y = y + jnp.dot(vt_ref[...], hprev.astype(BF16), preferred_element_type=F32)
    yt_ref[...] = y.astype(yt_ref.dtype)


def _s5_conv(at, wt, mt, vt, pr, pi, *, nc):
    g, k, n = at.shape
    levels = pr.shape[1]
    st2 = 2 * STATE
    return pl.pallas_call(
        functools.partial(_s5_conv_kernel, nc=nc, levels=levels),
        grid=(g,),
        in_specs=[
            pl.BlockSpec((None, k, n), lambda i: (i, 0, 0)),
            pl.BlockSpec((None, st2, k), lambda i: (i, 0, 0)),
            pl.BlockSpec((None, k, k), lambda i: (i, 0, 0)),
            pl.BlockSpec((None, k, st2), lambda i: (i, 0, 0)),
            pl.BlockSpec((None, levels, STATE, 1), lambda i: (i, 0, 0, 0)),
            pl.BlockSpec((None, levels, STATE, 1), lambda i: (i, 0, 0, 0)),
        ],
        out_specs=pl.BlockSpec((None, k, n), lambda i: (i, 0, 0)),
        out_shape=jax.ShapeDtypeStruct((g, k, n), BF16),
        compiler_params=_params("parallel"),
        name="s5_conv",
    )(at, wt, mt, vt, pr, pi)


def _s5_weights(lam_re, lam_im, log_dt, b_re, b_im, c_re, c_im, d_skip, nc):
    t = S5_CHUNK
    lr = jnp.minimum(lam_re.astype(F32), -1e-4)
    li = lam_im.astype(F32)
    dt = jnp.exp(log_dt.astype(F32))[:, None]
    mag = jnp.exp(lr * dt)
    a_re = mag * jnp.cos(li * dt)
    a_im = mag * jnp.sin(li * dt)
    den = lr * lr + li * li
    nr = a_re - 1.0
    f_re = (nr * lr + a_im * li) / den
    f_im = (a_im * lr - nr * li) / den
    br = b_re.astype(F32)
    bi = b_im.astype(F32)
    bb_re = f_re[..., None] * br - f_im[..., None] * bi
    bb_im = f_re[..., None] * bi + f_im[..., None] * br
    cr = c_re.astype(F32)
    ci = c_im.astype(F32)
    n_groups = cr.shape[0]

    def cpow(n):
        n = n.astype(F32)[:, None, None]
        m = jnp.exp(lr * dt * n)
        th = li * dt * n
        return m * jnp.cos(th), m * jnp.sin(th)

    pr, pi = cpow(jnp.arange(t + 1))
    car = cr[None] * pr[:, :, None, :] - ci[None] * pi[:, :, None, :]
    cai = cr[None] * pi[:, :, None, :] + ci[None] * pr[:, :, None, :]
    kk = (jnp.einsum('nghp,gpk->nghk', car[:t], bb_re) - jnp.einsum('nghp,gpk->nghk', cai[:t], bb_im))
    skip = d_skip.astype(F32).reshape(n_groups, GROUP_SIZE)
    kk = kk.at[0].add(skip[:, :, None] * jnp.eye(GROUP_SIZE, dtype=F32))
    lag = jnp.arange(t)[None, :] - jnp.arange(t)[:, None]
    sel = (lag[None] == jnp.arange(t)[:, None, None]).astype(F32)
    mt = jnp.einsum('nghk,nst->gthsk', kk, sel).reshape(n_groups, t * GROUP_SIZE, t * GROUP_SIZE)
    qr, qi = pr[:t][::-1], pi[:t][::-1]
    w_re = jnp.einsum('sgp,gpk->gpsk', qr, bb_re) - jnp.einsum('sgp,gpk->gpsk', qi, bb_im)
    w_im = jnp.einsum('sgp,gpk->gpsk', qr, bb_im) + jnp.einsum('sgp,gpk->gpsk', qi, bb_re)
    wt = jnp.concatenate([w_re, w_im], axis=1).reshape(n_groups, 2 * STATE, t * GROUP_SIZE)
    v_re = car[1:].transpose(1, 0, 2, 3)
    v_im = -cai[1:].transpose(1, 0, 2, 3)
    vt = jnp.concatenate([v_re, v_im], axis=-1).reshape(n_groups, t * GROUP_SIZE, 2 * STATE)
    levels = max(1, (nc - 1).bit_length())
    sr, si = cpow(t * (2 ** jnp.arange(levels)))
    return (wt.astype(BF16), mt.astype(BF16), vt.astype(BF16),
            sr.transpose(1, 0, 2)[..., None], si.transpose(1, 0, 2)[..., None])


def _tail(x, m, gpost, p, wpg, wpp):
    x1 = x + _rms(m, gpost)
    pg = _bdot(x1, wpg)
    pp = _bdot(p, wpp)
    return x1 + jax.nn.sigmoid(pg) * pp


def _s5_post_kernel(x_ref, yt_ref, gate_ref, p_ref, wglu_ref, bglu_ref, wout_ref,
                    gpost_ref, wpg_ref, wpp_ref, o_ref):
    n_groups = yt_ref.shape[0]
    yt = jnp.concatenate([yt_ref[g] for g in range(n_groups)], axis=0).astype(F32)
    g = jax.nn.gelu(yt).T
    z = _bdot(g, wglu_ref[...]) + bglu_ref[...]
    y = g * jax.nn.sigmoid(z)
    gate = gate_ref[...].astype(F32)
    y = y * (gate * jax.nn.sigmoid(gate))
    m = _bdot(y, wout_ref[...])
    o_ref[...] = _tail(x_ref[...], m, gpost_ref[...], p_ref[...], wpg_ref[...], wpp_ref[...])


def _s5_post(x3, yt, gate, p3, wglu, bglu, wout, gpost, wpg, wpp):
    bsz, nc, td = x3.shape
    d = td // S5_CHUNK
    pd = p3.shape[2] // S5_CHUNK
    n_groups = d // GROUP_SIZE
    const = lambda b, t: (0, 0)
    return pl.pallas_call(
        _s5_post_kernel,
        grid=(bsz, S5_CHUNK),
        in_specs=[
            pl.BlockSpec((None, nc, d), lambda b, t: (b, 0, t)),
            pl.BlockSpec((n_groups, GROUP_SIZE, nc), lambda b, t: (0, t, b)),
            pl.BlockSpec((nc, d), lambda b, t: (b * S5_CHUNK + t, 0)),
            pl.BlockSpec((None, nc, pd), lambda b, t: (b, 0, t)),
            pl.BlockSpec((d, d), const),
            pl.BlockSpec((1, d), const),
            pl.BlockSpec((d, d), const),
            pl.BlockSpec((1, d), const),
            pl.BlockSpec((d, d), const),
            pl.BlockSpec((pd, d), const),
        ],
        out_specs=pl.BlockSpec((None, nc, d), lambda b, t: (b, 0, t)),
        out_shape=jax.ShapeDtypeStruct((bsz, nc, td), F32),
        compiler_params=_params("parallel", "parallel"),
        name="s5_post",
    )(x3, yt, gate, p3, wglu, bglu.reshape(1, d), wout, gpost.reshape(1, d), wpg, wpp)


def _attn_post_kernel(x_ref, og_ref, p_ref, wout_ref, gpost_ref, wpg_ref, wpp_ref, o_ref):
    m = jnp.dot(og_ref[...], wout_ref[...], preferred_element_type=F32)
    o_ref[...] = _tail(x_ref[...], m, gpost_ref[...], p_ref[...], wpg_ref[...], wpp_ref[...])


def _attn_post(x, og, p, wout, gpost, wpg, wpp, *, tm):
    m, d = x.shape
    pd = p.shape[1]
    row = lambda i: (i, 0)
    const = lambda i: (0, 0)
    return pl.pallas_call(
        _attn_post_kernel,
        grid=(m // tm,),
        in_specs=[
            pl.BlockSpec((tm, d), row),
            pl.BlockSpec((tm, d), row),
            pl.BlockSpec((tm, pd), row),
            pl.BlockSpec((d, d), const),
            pl.BlockSpec((1, d), const),
            pl.BlockSpec((d, d), const),
            pl.BlockSpec((pd, d), const),
        ],
        out_specs=pl.BlockSpec((tm, d), row),
        out_shape=jax.ShapeDtypeStruct((m, d), F32),
        compiler_params=_params("parallel"),
        name="attn_post",
    )(x, og, p, wout, gpost.reshape(1, d), wpg, wpp)


ATTN_HEADS = 4
ATTN_TQ = 128
ATTN_TQ_OUTER = 1024
LOG_F32_UNDERFLOW = 104.0
ATTN_PAIR = 2
NO_KEYS_LEFT = 1e30


def _sb_attn_kernel(q_ref, gate_ref, k_ref, v_ref, o_ref, acc_ref, c_ref):
    tq, nh = ATTN_TQ, ATTN_HEADS
    lanes = nh * HEAD_DIM
    rows = nh * tq
    n_inner = q_ref.shape[0] // tq
    io = pl.program_id(2)
    lane_head = lax.broadcasted_iota(jnp.int32, (tq, lanes), 1) // HEAD_DIM
    kr = lax.broadcasted_iota(jnp.int32, (tq, 2 * tq), 0)
    kc = lax.broadcasted_iota(jnp.int32, (tq, 2 * tq), 1)
    tri_ones = jnp.where((kr > kc) | (kc >= tq), 1.0, 0.0).astype(BF16)
    qrow = lax.broadcasted_iota(jnp.int32, (rows, tq), 0) % tq
    kcol = lax.broadcasted_iota(jnp.int32, (rows, tq), 1)
    causal = kcol < qrow
    scale = jnp.asarray(HEAD_DIM ** -0.5, BF16)

    def tile(q4, j, c_prev, mask):
        st = pl.multiple_of(j * tq, tq)
        kj = k_ref[pl.ds(st, tq), :]
        vj = v_ref[pl.ds(st, tq), :]
        z = lax.dot_general(q4, kj, (((1,), (1,)), ((), ())), preferred_element_type=F32)
        sp = jnp.maximum(z, 0.0) + jnp.log(1.0 + jnp.exp(-jnp.abs(z)))
        if mask is not None:
            sp = jnp.where(mask, sp, 0.0)
        both = jnp.dot(sp.astype(BF16), tri_ones, preferred_element_type=F32)
        tail, rs = both[:, :tq], both[:, tq:]
        logw = (z - sp) - tail
        if c_prev is not None:
            logw = logw - c_prev
        w = jnp.exp(logw)
        if mask is not None:
            w = jnp.where(mask, w, 0.0)
        pv = jnp.dot(w.astype(BF16), vj, preferred_element_type=F32)
        return pv, rs

    def q_pair(qp, _):
        i0 = io * n_inner + ATTN_PAIR * qp
        q4s, mins = [], []
        for s in range(ATTN_PAIR):
            r0 = pl.multiple_of((ATTN_PAIR * qp + s) * tq, tq)
            q = q_ref[pl.ds(r0, tq), :] * scale
            q4 = jnp.concatenate([jnp.where(lane_head == h, q, jnp.zeros_like(q)) for h in range(nh)], axis=0)
            pv, rs = tile(q4, i0 + s, None, causal)
            acc_ref[s] = pv
            c_ref[s] = rs
            q4s.append(q4)
            mins.append(jnp.min(rs))

        def cond(state):
            n = state[0]
            unfinished = functools.reduce(jnp.logical_or, [m < LOG_F32_UNDERFLOW for m in state[1:]])
            return (i0 + (ATTN_PAIR - 2) - n >= 0) & unfinished

        def body(state):
            n = state[0]
            new_mins = []
            for s in range(ATTN_PAIR):
                j = i0 + s - 1 - n
                c_prev = c_ref[s]
                c_eff = jnp.where(j >= 0, c_prev, NO_KEYS_LEFT)
                pv, rs = tile(q4s[s], jnp.maximum(j, 0), c_eff, None)
                acc_ref[s] += pv
                c_new = c_prev + rs
                c_ref[s] = c_new
                new_mins.append(jnp.min(c_new))
            return (n + 1, *new_mins)

        lax.while_loop(cond, body, (jnp.int32(0), *mins))
        for s in range(ATTN_PAIR):
            r0 = pl.multiple_of((ATTN_PAIR * qp + s) * tq, tq)
            o = jnp.zeros((tq, lanes), F32)
            for h in range(nh):
                o = jnp.where(lane_head == h, acc_ref[s, h * tq:(h + 1) * tq, :], o)
            gate = gate_ref[pl.ds(r0, tq), :].astype(F32)
            o_ref[pl.ds(r0, tq), :] = (o * (gate * jax.nn.sigmoid(gate))).astype(o_ref.dtype)
        return 0

    lax.fori_loop(0, n_inner // ATTN_PAIR, q_pair, 0)


def _sb_attn(qg, kv):
    b, l, d2 = qg.shape
    d = d2 // 2
    lanes = ATTN_HEADS * HEAD_DIM
    nhg = d // lanes
    tqo = min(ATTN_TQ_OUTER, l)
    return pl.pallas_call(
        _sb_attn_kernel,
        grid=(b, nhg, l // tqo),
        in_specs=[
            pl.BlockSpec((None, tqo, lanes), lambda bi, hg, i: (bi, i, hg)),
            pl.BlockSpec((None, tqo, lanes), lambda bi, hg, i: (bi, i, nhg + hg)),
            pl.BlockSpec((None, l, lanes), lambda bi, hg, i: (bi, 0, hg)),
            pl.BlockSpec((None, l, lanes), lambda bi, hg, i: (bi, 0, nhg + hg)),
        ],
        out_specs=pl.BlockSpec((None, tqo, lanes), lambda bi, hg, i: (bi, i, hg)),
        out_shape=jax.ShapeDtypeStruct((b, l, d), BF16),
        scratch_shapes=[pltpu.VMEM((ATTN_PAIR, ATTN_HEADS * ATTN_TQ, lanes), F32),
                        pltpu.VMEM((ATTN_PAIR, ATTN_HEADS * ATTN_TQ, ATTN_TQ), F32)],
        compiler_params=_params("parallel", "parallel", "arbitrary"),
        name="sb_attn",
    )(qg, qg, kv, kv)


def kernel(x, p, a_norm_pre, a_norm_post, a_w_in, a_lam_re, a_lam_im, a_log_dt, a_b_re, a_b_im, a_c_re, a_c_im, a_d_skip, a_w_glu, a_b_glu, a_w_out, kv_norm, w_kv, b_norm_pre, b_norm_post, b_w_in, b_w_out, ple_w_proj, ple_w_gate):
    bsz, seq, d = x.shape
    depth = p.shape[0]
    n_a = a_w_in.shape[0]
    tokens = bsz * seq
    tm = 256
    nc = seq // S5_CHUNK
    xf = x.reshape(tokens, d)
    k_v = None
    for i in range(depth):
        wpg = ple_w_gate[i].astype(BF16)
        wpp = ple_w_proj[i].astype(BF16)
        if i < n_a:
            j = i
            x3 = xf.reshape(bsz, nc, S5_CHUNK * d)
            p3 = p[i].reshape(bsz, nc, -1)
            wut = a_w_in[j][:, :d].T.astype(BF16)
            wg = a_w_in[j][:, d:].astype(BF16)
            ut, gate = _s5_in_proj(x3, a_norm_pre[j], wut, wg)
            wt, mt, vt, pr, pi = _s5_weights(a_lam_re[j], a_lam_im[j], a_log_dt[j], a_b_re[j], a_b_im[j],
                                             a_c_re[j], a_c_im[j], a_d_skip[j], nc)
            yt = _s5_conv(ut, wt, mt, vt, pr, pi, nc=nc)
            xf = _s5_post(x3, yt, gate, p3, a_w_glu[j].astype(BF16), a_b_glu[j], a_w_out[j].astype(BF16),
                          a_norm_post[j], wpg, wpp).reshape(tokens, d)
        else:
            j = i - n_a
            pi_ = p[i].reshape(tokens, -1)
            qg = _norm_matmul(xf, b_norm_pre[j], b_w_in[j].astype(BF16), tm=tm, name="attn_in_proj")
            og = _sb_attn(qg.reshape(bsz, seq, 2 * d), k_v)
            xf = _attn_post(xf, og.reshape(tokens, d), pi_, b_w_out[j].astype(BF16), b_norm_post[j], wpg, wpp, tm=tm)
        if i == n_a - 1:
            k_v = _norm_matmul(xf, kv_norm, w_kv.astype(BF16), tm=tm, name="kv_proj").reshape(bsz, seq, 2 * d)
    return xf.reshape(bsz, seq, d)
```

```python
import functools

import jax
import jax.numpy as jnp
from jax import lax
from jax.experimental import pallas as pl
from jax.experimental.pallas import tpu as pltpu

F32 = jnp.float32
BF16 = jnp.bfloat16

EPS = 1e-6
GROUP_SIZE = 16
STATE = 64
HEAD_DIM = 64
S5_CHUNK = 16
VMEM_LIMIT = 56 * 1024 * 1024
PROJ_ROWS = 1024
POST_ROWS = 512


def _params(*sem):
    return pltpu.CompilerParams(dimension_semantics=sem, vmem_limit_bytes=VMEM_LIMIT)


def _rms(x, g):
    ms = jnp.mean(x * x, axis=-1, keepdims=True)
    return x * lax.rsqrt(ms + EPS) * g


def _bdot(a, b):
    return jnp.dot(a.astype(BF16), b, preferred_element_type=F32)


def _norm_matmul_kernel(x_ref, g_ref, w_ref, o_ref):
    h = _rms(x_ref[...], g_ref[...])
    o_ref[...] = _bdot(h, w_ref[...]).astype(o_ref.dtype)


def _norm_matmul(x, g, w, *, tm, name):
    m, d = x.shape
    n = w.shape[1]
    return pl.pallas_call(
        _norm_matmul_kernel,
        grid=(m // tm,),
        in_specs=[
            pl.BlockSpec((tm, d), lambda i: (i, 0)),
            pl.BlockSpec((1, d), lambda i: (0, 0)),
            pl.BlockSpec((d, n), lambda i: (0, 0)),
        ],
        out_specs=pl.BlockSpec((tm, n), lambda i: (i, 0)),
        out_shape=jax.ShapeDtypeStruct((m, n), BF16),
        compiler_params=_params("parallel"),
        name=name,
    )(x, g.reshape(1, d), w)


def _s5_in_proj_kernel(x_ref, g_ref, wut_ref, wg_ref, ut_ref, gate_ref):
    hn = _rms(x_ref[...], g_ref[...]).astype(BF16)
    ut = lax.dot_general(wut_ref[...], hn, (((1,), (1,)), ((), ())), preferred_element_type=F32)
    n_groups = ut_ref.shape[0]
    ut = ut.astype(BF16)
    for g in range(n_groups):
        ut_ref[g] = ut[g * GROUP_SIZE:(g + 1) * GROUP_SIZE, :]
    gate_ref[...] = jnp.dot(hn, wg_ref[...], preferred_element_type=F32).astype(gate_ref.dtype)


def _s5_in_proj(x3, g, wut, wg):
    bsz, nc, td = x3.shape
    d = td // S5_CHUNK
    n_groups = d // GROUP_SIZE
    return pl.pallas_call(
        _s5_in_proj_kernel,
        grid=(bsz, S5_CHUNK),
        in_specs=[
            pl.BlockSpec((None, nc, d), lambda b, t: (b, 0, t)),
            pl.BlockSpec((1, d), lambda b, t: (0, 0)),
            pl.BlockSpec((d, d), lambda b, t: (0, 0)),
            pl.BlockSpec((d, d), lambda b, t: (0, 0)),
        ],
        out_specs=[
            pl.BlockSpec((n_groups, GROUP_SIZE, nc), lambda b, t: (0, t, b)),
            pl.BlockSpec((nc, d), lambda b, t: (b * S5_CHUNK + t, 0)),
        ],
        out_shape=[
            jax.ShapeDtypeStruct((n_groups, S5_CHUNK * GROUP_SIZE, bsz * nc), BF16),
            jax.ShapeDtypeStruct((bsz * S5_CHUNK * nc, d), BF16),
        ],
        compiler_params=_params("parallel", "parallel"),
        name="s5_in_proj",
    )(x3, g.reshape(1, d), wut, wg)


S5_GROUPS_PER_STEP = 2


def _s5_conv_kernel(at_ref, wt_ref, mt_ref, vt_ref, ar_ref, ai_ref, yt_ref, *, nc):
    levels = max(1, (nc - 1).bit_length())
    for g in range(at_ref.shape[0]):
        at = at_ref[g]
        s = jnp.dot(wt_ref[g], at, preferred_element_type=F32)
        seg = lax.broadcasted_iota(jnp.int32, (STATE, s.shape[1]), 1) % nc
        hr, hi = s[:STATE], s[STATE:]
        pr, pi = ar_ref[g], ai_ref[g]
        for k in range(levels):
            d = 1 << k
            keep = seg >= d
            sr = jnp.where(keep, pltpu.roll(hr, d, axis=1), 0.0)
            si = jnp.where(keep, pltpu.roll(hi, d, axis=1), 0.0)
            hr, hi = hr + pr * sr - pi * si, hi + pr * si + pi * sr
            pr, pi = pr * pr - pi * pi, 2.0 * pr * pi
        keep = seg >= 1
        hprev = jnp.concatenate([jnp.where(keep, pltpu.roll(hr, 1, axis=1), 0.0),
                                 jnp.where(keep, pltpu.roll(hi, 1, axis=1), 0.0)], axis=0)
        y = jnp.dot(mt_ref[g], at, preferred_element_type=F32)
        y = y + jnp.dot(vt_ref[g], hprev.astype(BF16), preferred_element_type=F32)
        yt_ref[g] = y.astype(yt_ref.dtype)


def _s5_conv(at, wt, mt, vt, ar, ai, *, nc):
    g, k, n = at.shape
    st2 = 2 * STATE
    gs = S5_GROUPS_PER_STEP
    blk = lambda i: (i, 0, 0)
    return pl.pallas_call(
        functools.partial(_s5_conv_kernel, nc=nc),
        grid=(g // gs,),
        in_specs=[
            pl.BlockSpec((gs, k, n), blk),
            pl.BlockSpec((gs, st2, k), blk),
            pl.BlockSpec((gs, k, k), blk),
            pl.BlockSpec((gs, k, st2), blk),
            pl.BlockSpec((gs, STATE, 1), blk),
            pl.BlockSpec((gs, STATE, 1), blk),
        ],
        out_specs=pl.BlockSpec((gs, k, n), blk),
        out_shape=jax.ShapeDtypeStruct((g, k, n), BF16),
        compiler_params=_params("parallel"),
        name="s5_conv",
    )(at, wt, mt, vt, ar, ai)


def _s5_weights(lam_re, lam_im, log_dt, b_re, b_im, c_re, c_im, d_skip):
    t = S5_CHUNK
    lr = jnp.minimum(lam_re.astype(F32), -1e-4)
    li = lam_im.astype(F32)
    dt = jnp.exp(log_dt.astype(F32))[:, None]
    mag = jnp.exp(lr * dt)
    a_re = mag * jnp.cos(li * dt)
    a_im = mag * jnp.sin(li * dt)
    den = lr * lr + li * li
    nr = a_re - 1.0
    f_re = (nr * lr + a_im * li) / den
    f_im = (a_im * lr - nr * li) / den
    br = b_re.astype(F32)
    bi = b_im.astype(F32)
    bb_re = f_re[..., None] * br - f_im[..., None] * bi
    bb_im = f_re[..., None] * bi + f_im[..., None] * br
    cr = c_re.astype(F32)
    ci = c_im.astype(F32)
    n_groups = cr.shape[0]

    def cpow(n):
        n = n.astype(F32)[:, None, None]
        m = jnp.exp(lr * dt * n)
        th = li * dt * n
        return m * jnp.cos(th), m * jnp.sin(th)

    pr, pi = cpow(jnp.arange(t + 1))
    car = cr[None] * pr[:, :, None, :] - ci[None] * pi[:, :, None, :]
    cai = cr[None] * pi[:, :, None, :] + ci[None] * pr[:, :, None, :]
    kk = (jnp.einsum('nghp,gpk->nghk', car[:t], bb_re) - jnp.einsum('nghp,gpk->nghk', cai[:t], bb_im))
    skip = d_skip.astype(F32).reshape(n_groups, GROUP_SIZE)
    kk = kk.at[0].add(skip[:, :, None] * jnp.eye(GROUP_SIZE, dtype=F32))
    lag = jnp.arange(t)[None, :] - jnp.arange(t)[:, None]
    sel = (lag[None] == jnp.arange(t)[:, None, None]).astype(F32)
    mt = jnp.einsum('nghk,nst->gthsk', kk, sel).reshape(n_groups, t * GROUP_SIZE, t * GROUP_SIZE)
    qr, qi = pr[:t][::-1], pi[:t][::-1]
    w_re = jnp.einsum('sgp,gpk->gpsk', qr, bb_re) - jnp.einsum('sgp,gpk->gpsk', qi, bb_im)
    w_im = jnp.einsum('sgp,gpk->gpsk', qr, bb_im) + jnp.einsum('sgp,gpk->gpsk', qi, bb_re)
    wt = jnp.concatenate([w_re, w_im], axis=1).reshape(n_groups, 2 * STATE, t * GROUP_SIZE)
    v_re = car[1:].transpose(1, 0, 2, 3)
    v_im = -cai[1:].transpose(1, 0, 2, 3)
    vt = jnp.concatenate([v_re, v_im], axis=-1).reshape(n_groups, t * GROUP_SIZE, 2 * STATE)
    return wt.astype(BF16), mt.astype(BF16), vt.astype(BF16), pr[t][..., None], pi[t][..., None]


def _tail(x, m, gpost, p, wpg, wpp):
    x1 = x + _rms(m, gpost)
    pg = _bdot(x1, wpg)
    pp = _bdot(p, wpp)
    return x1 + jax.nn.sigmoid(pg) * pp


def _s5_post_kernel(x_ref, yt_ref, gate_ref, p_ref, wglu_ref, bglu_ref, wout_ref,
                    gpost_ref, wpg_ref, wpp_ref, o_ref):
    n_groups = yt_ref.shape[0]
    yt = jnp.concatenate([yt_ref[g] for g in range(n_groups)], axis=0).astype(F32)
    g = jax.nn.gelu(yt).T
    z = _bdot(g, wglu_ref[...]) + bglu_ref[...]
    y = g * jax.nn.sigmoid(z)
    gate = gate_ref[...].astype(F32)
    y = y * (gate * jax.nn.sigmoid(gate))
    m = _bdot(y, wout_ref[...])
    o_ref[...] = _tail(x_ref[...], m, gpost_ref[...], p_ref[...], wpg_ref[...], wpp_ref[...])


def _s5_post(x3, yt, gate, p3, wglu, bglu, wout, gpost, wpg, wpp):
    bsz, nc, td = x3.shape
    d = td // S5_CHUNK
    pd = p3.shape[2] // S5_CHUNK
    n_groups = d // GROUP_SIZE
    const = lambda b, t: (0, 0)
    return pl.pallas_call(
        _s5_post_kernel,
        grid=(bsz, S5_CHUNK),
        in_specs=[
            pl.BlockSpec((None, nc, d), lambda b, t: (b, 0, t)),
            pl.BlockSpec((n_groups, GROUP_SIZE, nc), lambda b, t: (0, t, b)),
            pl.BlockSpec((nc, d), lambda b, t: (b * S5_CHUNK + t, 0)),
            pl.BlockSpec((None, nc, pd), lambda b, t: (b, 0, t)),
            pl.BlockSpec((d, d), const),
            pl.BlockSpec((1, d), const),
            pl.BlockSpec((d, d), const),
            pl.BlockSpec((1, d), const),
            pl.BlockSpec((d, d), const),
            pl.BlockSpec((pd, d), const),
        ],
        out_specs=pl.BlockSpec((None, nc, d), lambda b, t: (b, 0, t)),
        out_shape=jax.ShapeDtypeStruct((bsz, nc, td), F32),
        compiler_params=_params("parallel", "parallel"),
        name="s5_post",
    )(x3, yt, gate, p3, wglu, bglu.reshape(1, d), wout, gpost.reshape(1, d), wpg, wpp)


def _attn_post_kernel(x_ref, og_ref, p_ref, wout_ref, gpost_ref, wpg_ref, wpp_ref, o_ref):
    m = jnp.dot(og_ref[...], wout_ref[...], preferred_element_type=F32)
    o_ref[...] = _tail(x_ref[...], m, gpost_ref[...], p_ref[...], wpg_ref[...], wpp_ref[...])


def _attn_post(x, og, p_all, layer, wout, gpost, wpg, wpp, *, tm):
    m, d = x.shape
    pd = p_all.shape[2]
    row = lambda i: (i, 0)
    const = lambda i: (0, 0)
    return pl.pallas_call(
        _attn_post_kernel,
        grid=(m // tm,),
        in_specs=[
            pl.BlockSpec((tm, d), row),
            pl.BlockSpec((tm, d), row),
            pl.BlockSpec((None, tm, pd), lambda i: (layer, i, 0)),
            pl.BlockSpec((d, d), const),
            pl.BlockSpec((1, d), const),
            pl.BlockSpec((d, d), const),
            pl.BlockSpec((pd, d), const),
        ],
        out_specs=pl.BlockSpec((tm, d), row),
        out_shape=jax.ShapeDtypeStruct((m, d), F32),
        compiler_params=_params("parallel"),
        name="attn_post",
    )(x, og, p_all, wout, gpost.reshape(1, d), wpg, wpp)


ATTN_HEADS = 4
ATTN_TQ = 128
ATTN_TQ_OUTER = 1024
LOG_F32_UNDERFLOW = 104.0
ATTN_PAIR = 2


def _sb_attn_kernel(q_ref, gate_ref, k_ref, v_ref, o_ref, acc_ref, c_ref):
    tq, nh = ATTN_TQ, ATTN_HEADS
    lanes = nh * HEAD_DIM
    rows = nh * tq
    n_inner = q_ref.shape[0] // tq
    io = pl.program_id(2)
    lane_head = lax.broadcasted_iota(jnp.int32, (tq, lanes), 1) // HEAD_DIM
    kr = lax.broadcasted_iota(jnp.int32, (tq, 2 * tq), 0)
    kc = lax.broadcasted_iota(jnp.int32, (tq, 2 * tq), 1)
    tri_ones = jnp.where((kr > kc) | (kc >= tq), 1.0, 0.0).astype(BF16)
    tw = 2 * tq
    tri_w = jnp.where(lax.broadcasted_iota(jnp.int32, (tw, tw), 0) > lax.broadcasted_iota(jnp.int32, (tw, tw), 1),
                      1.0, 0.0).astype(BF16)
    ones_w = jnp.ones((tw, tq), BF16)
    qrow =lax.broadcasted_iota(jnp.int32, (rows, tq), 0) % tq
    kcol = lax.broadcasted_iota(jnp.int32, (rows, tq), 1)
    causal = kcol < qrow
    scale = jnp.asarray(HEAD_DIM ** -0.5, BF16)

    def tile(q4, j, c_prev, mask):
        st = pl.multiple_of(j * tq, tq)
        kj = k_ref[pl.ds(st, tq), :]
        vj = v_ref[pl.ds(st, tq), :]
        z = lax.dot_general(q4, kj, (((1,), (1,)), ((), ())), preferred_element_type=F32)
        sp = jnp.maximum(z, 0.0) + jnp.log(1.0 + jnp.exp(-jnp.abs(z)))
        if mask is not None:
            sp = jnp.where(mask, sp, 0.0)
        both = jnp.dot(sp.astype(BF16), tri_ones, preferred_element_type=F32)
        tail, rs = both[:, :tq], both[:, tq:]
        logw = (z - sp) - tail
        if c_prev is not None:
            logw = logw - c_prev
        w = jnp.exp(logw)
        if mask is not None:
            w = jnp.where(mask, w, 0.0)
        pv = jnp.dot(w.astype(BF16), vj, preferred_element_type=F32)
        return pv, rs

    def window(q4, st, c_prev):
        kw = k_ref[pl.ds(st, tw), :]
        vw = v_ref[pl.ds(st, tw), :]
        z = lax.dot_general(q4, kw, (((1,), (1,)), ((), ())), preferred_element_type=F32)
        sp = jnp.maximum(z, 0.0) + jnp.log(1.0 + jnp.exp(-jnp.abs(z)))
        spb = sp.astype(BF16)
        tail = jnp.dot(spb, tri_w, preferred_element_type=F32)
        rs = jnp.dot(spb, ones_w, preferred_element_type=F32)
        w = jnp.exp((z - sp) - tail - jnp.concatenate([c_prev, c_prev], axis=1))
        pv = jnp.dot(w.astype(BF16), vw, preferred_element_type=F32)
        return pv, rs

    def q_pair(qp, _):
        i0 = io * n_inner + ATTN_PAIR * qp
        q4s, mins = [], []
        for s in range(ATTN_PAIR):
            r0 = pl.multiple_of((ATTN_PAIR * qp + s) * tq, tq)
            q = q_ref[pl.ds(r0, tq), :] * scale
            q4 = jnp.concatenate([jnp.where(lane_head == h, q, jnp.zeros_like(q)) for h in range(nh)], axis=0)
            pv, rs = tile(q4, i0 + s, None, causal)
            acc_ref[s] = pv
            c_ref[s] = rs
            q4s.append(q4)
            mins.append(jnp.min(rs))

        n_windows = i0 // 2

        def cond(state):
            n = state[0]
            unfinished = functools.reduce(jnp.logical_or, [m < LOG_F32_UNDERFLOW for m in state[1:]])
            return (n < n_windows) & unfinished

        def body(state):
            n = state[0]
            new_mins = []
            for s in range(ATTN_PAIR):
                st = pl.multiple_of((i0 + s) * tq - (n + 1) * tw, tq)
                c_prev = c_ref[s]
                pv, rs = window(q4s[s], st, c_prev)
                acc_ref[s] += pv
                c_new = c_prev + rs
                c_ref[s] = c_new
                new_mins.append(jnp.min(c_new))
            return (n + 1, *new_mins)

        final = lax.while_loop(cond, body, (jnp.int32(0), *mins))

        @pl.when((final[0] == n_windows) & (final[2] < LOG_F32_UNDERFLOW))
        def _():
            pv, _ = tile(q4s[1], 0, c_ref[1], None)
            acc_ref[1] += pv

        for s in range(ATTN_PAIR):
            r0 = pl.multiple_of((ATTN_PAIR * qp + s) * tq, tq)
            o = jnp.zeros((tq, lanes), F32)
            for h in range(nh):
                o = jnp.where(lane_head == h, acc_ref[s, h * tq:(h + 1) * tq, :], o)
            gate = gate_ref[pl.ds(r0, tq), :].astype(F32)
            o_ref[pl.ds(r0, tq), :] = (o * (gate * jax.nn.sigmoid(gate))).astype(o_ref.dtype)
        return 0

    lax.fori_loop(0, n_inner // ATTN_PAIR, q_pair, 0)


def _sb_attn(qg, kv):
    b, l, d2 = qg.shape
    d = d2 // 2
    lanes = ATTN_HEADS * HEAD_DIM
    nhg = d // lanes
    tqo = min(ATTN_TQ_OUTER, l)
    return pl.pallas_call(
        _sb_attn_kernel,
        grid=(b, nhg, l // tqo),
        in_specs=[
            pl.BlockSpec((None, tqo, lanes), lambda bi, hg, i: (bi, i, hg)),
            pl.BlockSpec((None, tqo, lanes), lambda bi, hg, i: (bi, i, nhg + hg)),
            pl.BlockSpec((None, l, lanes), lambda bi, hg, i: (bi, 0, hg)),
            pl.BlockSpec((None, l, lanes), lambda bi, hg, i: (bi, 0, nhg + hg)),
        ],
        out_specs=pl.BlockSpec((None, tqo, lanes), lambda bi, hg, i: (bi, i, hg)),
        out_shape=jax.ShapeDtypeStruct((b, l, d), BF16),
        scratch_shapes=[pltpu.VMEM((ATTN_PAIR, ATTN_HEADS * ATTN_TQ, lanes), F32),
                        pltpu.VMEM((ATTN_PAIR, ATTN_HEADS * ATTN_TQ, ATTN_TQ), F32)],
        compiler_params=_params("parallel", "parallel", "arbitrary"),
        name="sb_attn",
    )(qg, qg, kv, kv)


def kernel(x, p, a_norm_pre, a_norm_post, a_w_in, a_lam_re, a_lam_im, a_log_dt, a_b_re, a_b_im, a_c_re, a_c_im, a_d_skip, a_w_glu, a_b_glu, a_w_out, kv_norm, w_kv, b_norm_pre, b_norm_post, b_w_in, b_w_out, ple_w_proj, ple_w_gate):
    bsz, seq, d = x.shape
    depth = p.shape[0]
    n_a = a_w_in.shape[0]
    tokens = bsz * seq
    nc = seq // S5_CHUNK
    xf = x.reshape(tokens, d)
    p_all = p.reshape(depth, tokens, -1)
    k_v = None
    for i in range(depth):
        wpg = ple_w_gate[i].astype(BF16)
        wpp = ple_w_proj[i].astype(BF16)
        if i < n_a:
            j = i
            x3 = xf.reshape(bsz, nc, S5_CHUNK * d)
            p3 = p[i].reshape(bsz, nc, -1)
            wut = a_w_in[j][:, :d].T.astype(BF16)
            wg = a_w_in[j][:, d:].astype(BF16)
            ut, gate = _s5_in_proj(x3, a_norm_pre[j], wut, wg)
            wt, mt, vt, ar, ai = _s5_weights(a_lam_re[j], a_lam_im[j], a_log_dt[j], a_b_re[j], a_b_im[j],
                                             a_c_re[j], a_c_im[j], a_d_skip[j])
            yt = _s5_conv(ut, wt, mt, vt, ar, ai, nc=nc)
            xf = _s5_post(x3, yt, gate, p3, a_w_glu[j].astype(BF16), a_b_glu[j], a_w_out[j].astype(BF16),
                          a_norm_post[j], wpg, wpp).reshape(tokens, d)
        else:
            j = i - n_a
            qg = _norm_matmul(xf, b_norm_pre[j], b_w_in[j].astype(BF16), tm=PROJ_ROWS, name="attn_in_proj")
            og = _sb_attn(qg.reshape(bsz, seq, 2 * d), k_v)
            xf = _attn_post(xf, og.reshape(tokens, d), p_all, i, b_w_out[j].astype(BF16), b_norm_post[j], wpg, wpp,
                            tm=POST_ROWS)
        if i == n_a - 1:
            k_v = _norm_matmul(xf, kv_norm, w_kv.astype(BF16), tm=PROJ_ROWS, name="kv_proj").reshape(bsz, seq, 2 * d)
    return xf.reshape(bsz, seq, d)
```

```python
import functools

import jax
import jax.numpy as jnp
from jax import lax
from jax.experimental import pallas as pl
from jax.experimental.pallas import tpu as pltpu

F32 = jnp.float32
BF16 = jnp.bfloat16

EPS = 1e-6
GROUP_SIZE = 16
STATE = 64
HEAD_DIM = 64
S5_CHUNK = 16
VMEM_LIMIT = 56 * 1024 * 1024
PROJ_ROWS = 1024
POST_ROWS = 512


def _params(*sem):
    return pltpu.CompilerParams(dimension_semantics=sem, vmem_limit_bytes=VMEM_LIMIT)


def _rms(x, g):
    ms = jnp.mean(x * x, axis=-1, keepdims=True)
    return x * lax.rsqrt(ms + EPS) * g


def _bdot(a, b):
    return jnp.dot(a.astype(BF16), b, preferred_element_type=F32)


def _norm_matmul_kernel(x_ref, g_ref, w_ref, o_ref):
    h = _rms(x_ref[...], g_ref[...])
    o_ref[...] = _bdot(h, w_ref[...]).astype(o_ref.dtype)


def _norm_matmul(x, g, w, *, tm, name):
    m, d = x.shape
    n = w.shape[1]
    return pl.pallas_call(
        _norm_matmul_kernel,
        grid=(m // tm,),
        in_specs=[
            pl.BlockSpec((tm, d), lambda i: (i, 0)),
            pl.BlockSpec((1, d), lambda i: (0, 0)),
            pl.BlockSpec((d, n), lambda i: (0, 0)),
        ],
        out_specs=pl.BlockSpec((tm, n), lambda i: (i, 0)),
        out_shape=jax.ShapeDtypeStruct((m, n), BF16),
        compiler_params=_params("parallel"),
        name=name,
    )(x, g.reshape(1, d), w)


def _phase_rows(hbm, step, lead=()):
    return hbm.at[(*lead, step // S5_CHUNK, slice(None), step % S5_CHUNK, slice(None))]


def _fetch_phase_rows(hbm, buf, sem, k, n_steps, lead=()):
    slot = k % 2

    def copy(step, slot_):
        return pltpu.make_async_copy(_phase_rows(hbm, step, lead), buf.at[slot_], sem.at[slot_])

    @pl.when(k == 0)
    def _():
        copy(k, slot).start()

    @pl.when(k + 1 < n_steps)
    def _():
        copy(k + 1, 1 - slot).start()

    copy(k, slot).wait()
    return slot


def _s5_in_proj_kernel(x_hbm, g_ref, wut_ref, wg_ref, ut_ref, gate_ref, xbuf, sem, *, n_steps):
    slot = _fetch_phase_rows(x_hbm, xbuf, sem, pl.program_id(0), n_steps)
    hn = _rms(xbuf[slot], g_ref[...]).astype(BF16)
    ut = lax.dot_general(wut_ref[...], hn, (((1,), (1,)), ((), ())), preferred_element_type=F32)
    n_groups = ut_ref.shape[0]
    ut = ut.astype(BF16)
    for g in range(n_groups):
        ut_ref[g] = ut[g * GROUP_SIZE:(g + 1) * GROUP_SIZE, :]
    gate_ref[...] = jnp.dot(hn, wg_ref[...], preferred_element_type=F32).astype(gate_ref.dtype)


def _s5_in_proj(x4, g, wut, wg):
    bsz, nc, _, d = x4.shape
    n_groups = d // GROUP_SIZE
    n_steps = bsz * S5_CHUNK
    const = lambda k: (0, 0)
    return pl.pallas_call(
        functools.partial(_s5_in_proj_kernel, n_steps=n_steps),
        grid=(n_steps,),
        in_specs=[
            pl.BlockSpec(memory_space=pl.ANY),
            pl.BlockSpec((1, d), const),
            pl.BlockSpec((d, d), const),
            pl.BlockSpec((d, d), const),
        ],
        out_specs=[
            pl.BlockSpec((n_groups, GROUP_SIZE, nc), lambda k: (0, k % S5_CHUNK, k // S5_CHUNK)),
            pl.BlockSpec((nc, d), lambda k: (k, 0)),
        ],
        out_shape=[
            jax.ShapeDtypeStruct((n_groups, S5_CHUNK * GROUP_SIZE, bsz * nc), BF16),
            jax.ShapeDtypeStruct((n_steps * nc, d), BF16),
        ],
        scratch_shapes=[pltpu.VMEM((2, nc, d), F32), pltpu.SemaphoreType.DMA((2,))],
        compiler_params=_params("arbitrary"),
        name="s5_in_proj",
    )(x4, g.reshape(1, d), wut, wg)


S5_GROUPS_PER_STEP = 2


def _s5_conv_kernel(at_ref, wt_ref, mt_ref, vt_ref, ar_ref, ai_ref, yt_ref, *, nc):
    levels = max(1, (nc - 1).bit_length())
    for g in range(at_ref.shape[0]):
        at = at_ref[g]
        s = jnp.dot(wt_ref[g], at, preferred_element_type=F32)
        seg = lax.broadcasted_iota(jnp.int32, (STATE, s.shape[1]), 1) % nc
        hr, hi = s[:STATE], s[STATE:]
        pr, pi = ar_ref[g], ai_ref[g]
        for k in range(levels):
            d = 1 << k
            keep = seg >= d
            sr = jnp.where(keep, pltpu.roll(hr, d, axis=1), 0.0)
            si = jnp.where(keep, pltpu.roll(hi, d, axis=1), 0.0)
            hr, hi = hr + pr * sr - pi * si, hi + pr * si + pi * sr
            pr, pi = pr * pr - pi * pi, 2.0 * pr * pi
        keep = seg >= 1
        hprev = jnp.concatenate([jnp.where(keep, pltpu.roll(hr, 1, axis=1), 0.0),
                                 jnp.where(keep, pltpu.roll(hi, 1, axis=1), 0.0)], axis=0)
        y = jnp.dot(mt_ref[g], at, preferred_element_type=F32)
        y = y + jnp.dot(vt_ref[g], hprev.astype(BF16), preferred_element_type=F32)
        yt_ref[g] = y.astype(yt_ref.dtype)


def _s5_conv(at, wt, mt, vt, ar, ai, *, nc):
    g, k, n = at.shape
    st2 = 2 * STATE
    gs = S5_GROUPS_PER_STEP
    blk = lambda i: (i, 0, 0)
    return pl.pallas_call(
        functools.partial(_s5_conv_kernel, nc=nc),
        grid=(g // gs,),
        in_specs=[
            pl.BlockSpec((gs, k, n), blk),
            pl.BlockSpec((gs, st2, k), blk),
            pl.BlockSpec((gs, k, k), blk),
            pl.BlockSpec((gs, k, st2), blk),
            pl.BlockSpec((gs, STATE, 1), blk),
            pl.BlockSpec((gs, STATE, 1), blk),
        ],
        out_specs=pl.BlockSpec((gs, k, n), blk),
        out_shape=jax.ShapeDtypeStruct((g, k, n), BF16),
        compiler_params=_params("parallel"),
        name="s5_conv",
    )(at, wt, mt, vt, ar, ai)


def _s5_weights(lam_re, lam_im, log_dt, b_re, b_im, c_re, c_im, d_skip):
    t = S5_CHUNK
    lr = jnp.minimum(lam_re.astype(F32), -1e-4)
    li = lam_im.astype(F32)
    dt = jnp.exp(log_dt.astype(F32))[:, None]
    mag = jnp.exp(lr * dt)
    a_re = mag * jnp.cos(li * dt)
    a_im = mag * jnp.sin(li * dt)
    den = lr * lr + li * li
    nr = a_re - 1.0
    f_re = (nr * lr + a_im * li) / den
    f_im = (a_im * lr - nr * li) / den
    br = b_re.astype(F32)
    bi = b_im.astype(F32)
    bb_re = f_re[..., None] * br - f_im[..., None] * bi
    bb_im = f_re[..., None] * bi + f_im[..., None] * br
    cr = c_re.astype(F32)
    ci = c_im.astype(F32)
    n_groups = cr.shape[0]

    def cpow(n):
        n = n.astype(F32)[:, None, None]
        m = jnp.exp(lr * dt * n)
        th = li * dt * n
        return m * jnp.cos(th), m * jnp.sin(th)

    pr, pi = cpow(jnp.arange(t + 1))
    car = cr[None] * pr[:, :, None, :] - ci[None] * pi[:, :, None, :]
    cai = cr[None] * pi[:, :, None, :] + ci[None] * pr[:, :, None, :]
    kk = (jnp.einsum('nghp,gpk->nghk', car[:t], bb_re) - jnp.einsum('nghp,gpk->nghk', cai[:t], bb_im))
    skip = d_skip.astype(F32).reshape(n_groups, GROUP_SIZE)
    kk = kk.at[0].add(skip[:, :, None] * jnp.eye(GROUP_SIZE, dtype=F32))
    lag = jnp.arange(t)[None, :] - jnp.arange(t)[:, None]
    sel = (lag[None] == jnp.arange(t)[:, None, None]).astype(F32)
    mt = jnp.einsum('nghk,nst->gthsk', kk, sel).reshape(n_groups, t * GROUP_SIZE, t * GROUP_SIZE)
    qr, qi = pr[:t][::-1], pi[:t][::-1]
    w_re = jnp.einsum('sgp,gpk->gpsk', qr, bb_re) - jnp.einsum('sgp,gpk->gpsk', qi, bb_im)
    w_im = jnp.einsum('sgp,gpk->gpsk', qr, bb_im) + jnp.einsum('sgp,gpk->gpsk', qi, bb_re)
    wt = jnp.concatenate([w_re, w_im], axis=1).reshape(n_groups, 2 * STATE, t * GROUP_SIZE)
    v_re = car[1:].transpose(1, 0, 2, 3)
    v_im = -cai[1:].transpose(1, 0, 2, 3)
    vt = jnp.concatenate([v_re, v_im], axis=-1).reshape(n_groups, t * GROUP_SIZE, 2 * STATE)
    return wt.astype(BF16), mt.astype(BF16), vt.astype(BF16), pr[t][..., None], pi[t][..., None]


def _tail(x, m, gpost, p, wpg, wpp):
    x1 = x + _rms(m, gpost)
    pg = _bdot(x1, wpg)
    pp = _bdot(p, wpp)
    return x1 + jax.nn.sigmoid(pg) * pp


def _s5_post_kernel(x_hbm, p_hbm, yt_ref, gate_ref, wglu_ref, bglu_ref, wout_ref, gpost_ref, wpg_ref, wpp_ref,
                    o_hbm, xbuf, pbuf, obuf, sem_x, sem_p, sem_o, *, n_steps, layer):
    k = pl.program_id(0)
    slot = _fetch_phase_rows(x_hbm, xbuf, sem_x, k, n_steps)
    _fetch_phase_rows(p_hbm, pbuf, sem_p, k, n_steps, lead=(layer,))
    n_groups = yt_ref.shape[0]
    yt = jnp.concatenate([yt_ref[g] for g in range(n_groups)], axis=0).astype(F32)
    g = jax.nn.gelu(yt).T
    z = _bdot(g, wglu_ref[...]) + bglu_ref[...]
    y = g * jax.nn.sigmoid(z)
    gate = gate_ref[...].astype(F32)
    y = y * (gate * jax.nn.sigmoid(gate))
    m = _bdot(y, wout_ref[...])
    res = _tail(xbuf[slot], m, gpost_ref[...], pbuf[slot], wpg_ref[...], wpp_ref[...])

    def out_copy(step, slot_):
        return pltpu.make_async_copy(obuf.at[slot_], _phase_rows(o_hbm, step), sem_o.at[slot_])

    @pl.when(k >= 2)
    def _():
        out_copy(k - 2, slot).wait()

    obuf[slot] = res
    out_copy(k, slot).start()

    @pl.when(k == n_steps - 1)
    def _():
        if n_steps >= 2:
            out_copy(k - 1, 1 - slot).wait()
        out_copy(k, slot).wait()


def _s5_post(x4, yt, gate, p5, layer, wglu, bglu, wout, gpost, wpg, wpp):
    bsz, nc, _, d = x4.shape
    pd = p5.shape[-1]
    n_groups = d // GROUP_SIZE
    n_steps = bsz * S5_CHUNK
    const = lambda k: (0, 0)
    return pl.pallas_call(
        functools.partial(_s5_post_kernel, n_steps=n_steps, layer=layer),
        grid=(n_steps,),
        in_specs=[
            pl.BlockSpec(memory_space=pl.ANY),
            pl.BlockSpec(memory_space=pl.ANY),
            pl.BlockSpec((n_groups, GROUP_SIZE, nc), lambda k: (0, k % S5_CHUNK, k // S5_CHUNK)),
            pl.BlockSpec((nc, d), lambda k: (k, 0)),
            pl.BlockSpec((d, d), const),
            pl.BlockSpec((1, d), const),
            pl.BlockSpec((d, d), const),
            pl.BlockSpec((1, d), const),
            pl.BlockSpec((d, d), const),
            pl.BlockSpec((pd, d), const),
        ],
        out_specs=pl.BlockSpec(memory_space=pl.ANY),
        out_shape=jax.ShapeDtypeStruct((bsz, nc, S5_CHUNK, d), F32),
        scratch_shapes=[pltpu.VMEM((2, nc, d), F32), pltpu.VMEM((2, nc, pd), F32), pltpu.VMEM((2, nc, d), F32),
                        pltpu.SemaphoreType.DMA((2,)), pltpu.SemaphoreType.DMA((2,)), pltpu.SemaphoreType.DMA((2,))],
        compiler_params=_params("arbitrary"),
        name="s5_post",
    )(x4, p5, yt, gate, wglu, bglu.reshape(1, d), wout, gpost.reshape(1, d), wpg, wpp)


def _attn_post_kernel(x_ref, og_ref, p_ref, wout_ref, gpost_ref, wpg_ref, wpp_ref, o_ref):
    m = jnp.dot(og_ref[...], wout_ref[...], preferred_element_type=F32)
    o_ref[...] = _tail(x_ref[...], m, gpost_ref[...], p_ref[...], wpg_ref[...], wpp_ref[...])


def _attn_post(x, og, p_all, layer, wout, gpost, wpg, wpp, *, tm):
    m, d = x.shape
    pd = p_all.shape[2]
    row = lambda i: (i, 0)
    const = lambda i: (0, 0)
    return pl.pallas_call(
        _attn_post_kernel,
        grid=(m // tm,),
        in_specs=[
            pl.BlockSpec((tm, d), row),
            pl.BlockSpec((tm, d), row),
            pl.BlockSpec((None, tm, pd), lambda i: (layer, i, 0)),
            pl.BlockSpec((d, d), const),
            pl.BlockSpec((1, d), const),
            pl.BlockSpec((d, d), const),
            pl.BlockSpec((pd, d), const),
        ],
        out_specs=pl.BlockSpec((tm, d), row),
        out_shape=jax.ShapeDtypeStruct((m, d), F32),
        compiler_params=_params("parallel"),
        name="attn_post",
    )(x, og, p_all, wout, gpost.reshape(1, d), wpg, wpp)


ATTN_HEADS = 4
ATTN_TQ = 128
ATTN_TQ_OUTER = 1024
LOG_F32_UNDERFLOW = 104.0
ATTN_PAIR = 2


def _sb_attn_kernel(q_ref, gate_ref, k_ref, v_ref, o_ref, acc_ref, c_ref):
    tq, nh = ATTN_TQ, ATTN_HEADS
    lanes = nh * HEAD_DIM
    rows = nh * tq
    n_inner = q_ref.shape[0] // tq
    io = pl.program_id(2)
    lane_head = lax.broadcasted_iota(jnp.int32, (tq, lanes), 1) // HEAD_DIM
    kr = lax.broadcasted_iota(jnp.int32, (tq, 2 * tq), 0)
    kc = lax.broadcasted_iota(jnp.int32, (tq, 2 * tq), 1)
    tri_ones = jnp.where((kr > kc) | (kc >= tq), 1.0, 0.0).astype(BF16)
    tw = 2 * tq
    tri_w = jnp.where(lax.broadcasted_iota(jnp.int32, (tw, tw), 0) > lax.broadcasted_iota(jnp.int32, (tw, tw), 1),
                      1.0, 0.0).astype(BF16)
    ones_w = jnp.ones((tw, tq), BF16)
    qrow =lax.broadcasted_iota(jnp.int32, (rows, tq), 0) % tq
    kcol = lax.broadcasted_iota(jnp.int32, (rows, tq), 1)
    causal = kcol < qrow
    scale = jnp.asarray(HEAD_DIM ** -0.5, BF16)

    def tile(q4, j, c_prev, mask):
        st = pl.multiple_of(j * tq, tq)
        kj = k_ref[pl.ds(st, tq), :]
        vj = v_ref[pl.ds(st, tq), :]
        z = lax.dot_general(q4, kj, (((1,), (1,)), ((), ())), preferred_element_type=F32)
        sp = jnp.maximum(z, 0.0) + jnp.log(1.0 + jnp.exp(-jnp.abs(z)))
        if mask is not None:
            sp = jnp.where(mask, sp, 0.0)
        both = jnp.dot(sp.astype(BF16), tri_ones, preferred_element_type=F32)
        tail, rs = both[:, :tq], both[:, tq:]
        logw = (z - sp) - tail
        if c_prev is not None:
            logw = logw - c_prev
        w = jnp.exp(logw)
        if mask is not None:
            w = jnp.where(mask, w, 0.0)
        pv = jnp.dot(w.astype(BF16), vj, preferred_element_type=F32)
        return pv, rs

    def window(q4, st, c_prev):
        kw = k_ref[pl.ds(st, tw), :]
        vw = v_ref[pl.ds(st, tw), :]
        z = lax.dot_general(q4, kw, (((1,), (1,)), ((), ())), preferred_element_type=F32)
        sp = jnp.maximum(z, 0.0) + jnp.log(1.0 + jnp.exp(-jnp.abs(z)))
        spb = sp.astype(BF16)
        tail = jnp.dot(spb, tri_w, preferred_element_type=F32)
        rs = jnp.dot(spb, ones_w, preferred_element_type=F32)
        w = jnp.exp((z - sp) - tail - jnp.concatenate([c_prev, c_prev], axis=1))
        pv = jnp.dot(w.astype(BF16), vw, preferred_element_type=F32)
        return pv, rs

    def q_pair(qp, _):
        i0 = io * n_inner + ATTN_PAIR * qp
        q4s, mins = [], []
        for s in range(ATTN_PAIR):
            r0 = pl.multiple_of((ATTN_PAIR * qp + s) * tq, tq)
            q = q_ref[pl.ds(r0, tq), :] * scale
            q4 = jnp.concatenate([jnp.where(lane_head == h, q, jnp.zeros_like(q)) for h in range(nh)], axis=0)
            pv, rs = tile(q4, i0 + s, None, causal)
            acc_ref[s] = pv
            c_ref[s] = rs
            q4s.append(q4)
            mins.append(jnp.min(rs))

        n_windows = i0 // 2

        def cond(state):
            n = state[0]
            unfinished = functools.reduce(jnp.logical_or, [m < LOG_F32_UNDERFLOW for m in state[1:]])
            return (n < n_windows) & unfinished

        def body(state):
            n = state[0]
            new_mins = []
            for s in range(ATTN_PAIR):
                st = pl.multiple_of((i0 + s) * tq - (n + 1) * tw, tq)
                c_prev = c_ref[s]
                pv, rs = window(q4s[s], st, c_prev)
                acc_ref[s] += pv
                c_new = c_prev + rs
                c_ref[s] = c_new
                new_mins.append(jnp.min(c_new))
            return (n + 1, *new_mins)

        final = lax.while_loop(cond, body, (jnp.int32(0), *mins))

        @pl.when((final[0] == n_windows) & (final[2] < LOG_F32_UNDERFLOW))
        def _():
            pv, _ = tile(q4s[1], 0, c_ref[1], None)
            acc_ref[1] += pv

        for s in range(ATTN_PAIR):
            r0 = pl.multiple_of((ATTN_PAIR * qp + s) * tq, tq)
            o = jnp.zeros((tq, lanes), F32)
            for h in range(nh):
                o = jnp.where(lane_head == h, acc_ref[s, h * tq:(h + 1) * tq, :], o)
            gate = gate_ref[pl.ds(r0, tq), :].astype(F32)
            o_ref[pl.ds(r0, tq), :] = (o * (gate * jax.nn.sigmoid(gate))).astype(o_ref.dtype)
        return 0

    lax.fori_loop(0, n_inner // ATTN_PAIR, q_pair, 0)


def _sb_attn(qg, kv):
    b, l, d2 = qg.shape
    d = d2 // 2
    lanes = ATTN_HEADS * HEAD_DIM
    nhg = d // lanes
    tqo = min(ATTN_TQ_OUTER, l)
    return pl.pallas_call(
        _sb_attn_kernel,
        grid=(b, nhg, l // tqo),
        in_specs=[
            pl.BlockSpec((None, tqo, lanes), lambda bi, hg, i: (bi, i, hg)),
            pl.BlockSpec((None, tqo, lanes), lambda bi, hg, i: (bi, i, nhg + hg)),
            pl.BlockSpec((None, l, lanes), lambda bi, hg, i: (bi, 0, hg)),
            pl.BlockSpec((None, l, lanes), lambda bi, hg, i: (bi, 0, nhg + hg)),
        ],
        out_specs=pl.BlockSpec((None, tqo, lanes), lambda bi, hg, i: (bi, i, hg)),
        out_shape=jax.ShapeDtypeStruct((b, l, d), BF16),
        scratch_shapes=[pltpu.VMEM((ATTN_PAIR, ATTN_HEADS * ATTN_TQ, lanes), F32),
                        pltpu.VMEM((ATTN_PAIR, ATTN_HEADS * ATTN_TQ, ATTN_TQ), F32)],
        compiler_params=_params("parallel", "parallel", "arbitrary"),
        name="sb_attn",
    )(qg, qg, kv, kv)


def kernel(x, p, a_norm_pre, a_norm_post, a_w_in, a_lam_re, a_lam_im, a_log_dt, a_b_re, a_b_im, a_c_re, a_c_im, a_d_skip, a_w_glu, a_b_glu, a_w_out, kv_norm, w_kv, b_norm_pre, b_norm_post, b_w_in, b_w_out, ple_w_proj, ple_w_gate):
    bsz, seq, d = x.shape
    depth = p.shape[0]
    n_a = a_w_in.shape[0]
    tokens = bsz * seq
    nc = seq // S5_CHUNK
    xf = x.reshape(tokens, d)
    p_all = p.reshape(depth, tokens, -1)
    k_v = None
    for i in range(depth):
        wpg = ple_w_gate[i].astype(BF16)
        wpp = ple_w_proj[i].astype(BF16)
        if i < n_a:
            j = i
            x4 = xf.reshape(bsz, nc, S5_CHUNK, d)
            p5 = p.reshape(depth, bsz, nc, S5_CHUNK, -1)
            wut = a_w_in[j][:, :d].T.astype(BF16)
            wg = a_w_in[j][:, d:].astype(BF16)
            ut, gate = _s5_in_proj(x4, a_norm_pre[j], wut, wg)
            wt, mt, vt, ar, ai = _s5_weights(a_lam_re[j], a_lam_im[j], a_log_dt[j], a_b_re[j], a_b_im[j],
                                             a_c_re[j], a_c_im[j], a_d_skip[j])
            yt = _s5_conv(ut, wt, mt, vt, ar, ai, nc=nc)
            xf = _s5_post(x4, yt, gate, p5, i, a_w_glu[j].astype(BF16), a_b_glu[j], a_w_out[j].astype(BF16),
                          a_norm_post[j], wpg, wpp).reshape(tokens, d)
        else:
            j = i - n_a
            qg = _norm_matmul(xf, b_norm_pre[j], b_w_in[j].astype(BF16), tm=PROJ_ROWS, name="attn_in_proj")
            og = _sb_attn(qg.reshape(bsz, seq, 2 * d), k_v)
            xf = _attn_post(xf, og.reshape(tokens, d), p_all, i, b_w_out[j].astype(BF16), b_norm_post[j], wpg, wpp,
                            tm=POST_ROWS)
        if i == n_a - 1:
            k_v = _norm_matmul(xf, kv_norm, w_kv.astype(BF16), tm=PROJ_ROWS, name="kv_proj").reshape(bsz, seq, 2 * d)
    return xf.reshape(bsz, seq, d)
```

```python
import functools

import jax
import jax.numpy as jnp
from jax import lax
from jax.experimental import pallas as pl
from jax.experimental.pallas import tpu as pltpu

F32 = jnp.float32
BF16 = jnp.bfloat16

EPS = 1e-6
GROUP_SIZE = 16
STATE = 64
HEAD_DIM = 64
S5_CHUNK = 16
VMEM_LIMIT = 56 * 1024 * 1024
PROJ_ROWS = 1024
POST_ROWS = 512


def _params(*sem):
    return pltpu.CompilerParams(dimension_semantics=sem, vmem_limit_bytes=VMEM_LIMIT)


def _rms(x, g):
    ms = jnp.mean(x * x, axis=-1, keepdims=True)
    return x * lax.rsqrt(ms + EPS) * g


def _bdot(a, b):
    return jnp.dot(a.astype(BF16), b, preferred_element_type=F32)


def _norm_matmul_kernel(x_ref, g_ref, w_ref, o_ref):
    h = _rms(x_ref[...], g_ref[...])
    o_ref[...] = _bdot(h, w_ref[...]).astype(o_ref.dtype)


def _norm_matmul(x, g, w, *, tm, name):
    m, d = x.shape
    n = w.shape[1]
    return pl.pallas_call(
        _norm_matmul_kernel,
        grid=(m // tm,),
        in_specs=[
            pl.BlockSpec((tm, d), lambda i: (i, 0)),
            pl.BlockSpec((1, d), lambda i: (0, 0)),
            pl.BlockSpec((d, n), lambda i: (0, 0)),
        ],
        out_specs=pl.BlockSpec((tm, n), lambda i: (i, 0)),
        out_shape=jax.ShapeDtypeStruct((m, n), BF16),
        compiler_params=_params("parallel"),
        name=name,
    )(x, g.reshape(1, d), w)


def _phase_rows(hbm, step, lead=()):
    return hbm.at[(*lead, step // S5_CHUNK, slice(None), step % S5_CHUNK, slice(None))]


def _fetch_phase_rows(hbm, buf, sem, k, n_steps, lead=()):
    slot = k % 2

    def copy(step, slot_):
        return pltpu.make_async_copy(_phase_rows(hbm, step, lead), buf.at[slot_], sem.at[slot_])

    @pl.when(k == 0)
    def _():
        copy(k, slot).start()

    @pl.when(k + 1 < n_steps)
    def _():
        copy(k + 1, 1 - slot).start()

    copy(k, slot).wait()
    return slot


def _s5_in_proj_kernel(x_hbm, g_ref, wut_ref, wg_ref, ut_ref, gate_ref, xbuf, sem, *, n_steps):
    slot = _fetch_phase_rows(x_hbm, xbuf, sem, pl.program_id(0), n_steps)
    hn = _rms(xbuf[slot], g_ref[...]).astype(BF16)
    ut = lax.dot_general(wut_ref[...], hn, (((1,), (1,)), ((), ())), preferred_element_type=F32)
    n_groups = ut_ref.shape[0]
    ut = ut.astype(BF16)
    for g in range(n_groups):
        ut_ref[g] = ut[g * GROUP_SIZE:(g + 1) * GROUP_SIZE, :]
    gate_ref[...] = jnp.dot(hn, wg_ref[...], preferred_element_type=F32).astype(gate_ref.dtype)


def _s5_in_proj(x4, g, wut, wg):
    bsz, nc, _, d = x4.shape
    n_groups = d // GROUP_SIZE
    n_steps = bsz * S5_CHUNK
    const = lambda k: (0, 0)
    return pl.pallas_call(
        functools.partial(_s5_in_proj_kernel, n_steps=n_steps),
        grid=(n_steps,),
        in_specs=[
            pl.BlockSpec(memory_space=pl.ANY),
            pl.BlockSpec((1, d), const),
            pl.BlockSpec((d, d), const),
            pl.BlockSpec((d, d), const),
        ],
        out_specs=[
            pl.BlockSpec((n_groups, GROUP_SIZE, nc), lambda k: (0, k % S5_CHUNK, k // S5_CHUNK)),
            pl.BlockSpec((nc, d), lambda k: (k, 0)),
        ],
        out_shape=[
            jax.ShapeDtypeStruct((n_groups, S5_CHUNK * GROUP_SIZE, bsz * nc), BF16),
            jax.ShapeDtypeStruct((n_steps * nc, d), BF16),
        ],
        scratch_shapes=[pltpu.VMEM((2, nc, d), F32), pltpu.SemaphoreType.DMA((2,))],
        compiler_params=_params("arbitrary"),
        name="s5_in_proj",
    )(x4, g.reshape(1, d), wut, wg)


S5_PAIR = 2


def _s5_conv_kernel(at_ref, wt_ref, kr_ref, vre_ref, vim_ref, ar_ref, ai_ref, yt_ref, *, nc):
    levels = max(1, (nc - 1).bit_length())
    kdim = at_ref.shape[1]
    s = [jnp.dot(wt_ref[q], at_ref[q], preferred_element_type=F32) for q in range(S5_PAIR)]
    hr = jnp.concatenate([sq[:STATE] for sq in s], axis=0).T
    hi = jnp.concatenate([sq[STATE:] for sq in s], axis=0).T
    seg = lax.broadcasted_iota(jnp.int32, hr.shape, 0) % nc
    pr, pi = ar_ref[...], ai_ref[...]
    for k in range(levels):
        d = 1 << k
        keep = seg >= d
        sr = jnp.where(keep, pltpu.roll(hr, d, axis=0), 0.0)
        si = jnp.where(keep, pltpu.roll(hi, d, axis=0), 0.0)
        hr, hi = hr + pr * sr - pi * si, hi + pr * si + pi * sr
        pr, pi = pr * pr - pi * pi, 2.0 * pr * pi
    keep = seg >= 1
    hr = jnp.where(keep, pltpu.roll(hr, 1, axis=0), 0.0).astype(BF16)
    hi = jnp.where(keep, pltpu.roll(hi, 1, axis=0), 0.0).astype(BF16)
    lane_s = lax.broadcasted_iota(jnp.int32, (kdim, kdim), 1) // GROUP_SIZE
    nt = (((1,), (1,)), ((), ()))
    for q in range(S5_PAIR):
        krep = kr_ref[q]
        mt = jnp.zeros_like(krep)
        for sblk in range(S5_CHUNK):
            r = sblk * GROUP_SIZE
            shifted = krep if r == 0 else jnp.concatenate([jnp.zeros((r, kdim), krep.dtype), krep[:kdim - r]], axis=0)
            mt = jnp.where(lane_s == sblk, shifted, mt)
        y = jnp.dot(mt, at_ref[q], preferred_element_type=F32)
        y = y + lax.dot_general(vre_ref[q], hr, nt, preferred_element_type=F32)
        y = y + lax.dot_general(vim_ref[q], hi, nt, preferred_element_type=F32)
        yt_ref[q] = y.astype(yt_ref.dtype)


def _s5_conv(at, wt, krep, vre, vim, ar, ai, *, nc):
    g, k, n = at.shape
    st2 = 2 * STATE
    gs = S5_PAIR
    blk = lambda i: (i, 0, 0)
    return pl.pallas_call(
        functools.partial(_s5_conv_kernel, nc=nc),
        grid=(g // gs,),
        in_specs=[
            pl.BlockSpec((gs, k, n), blk),
            pl.BlockSpec((gs, st2, k), blk),
            pl.BlockSpec((gs, k, k), blk),
            pl.BlockSpec((gs, k, st2), blk),
            pl.BlockSpec((gs, k, st2), blk),
            pl.BlockSpec((None, 1, st2), blk),
            pl.BlockSpec((None, 1, st2), blk),
        ],
        out_specs=pl.BlockSpec((gs, k, n), blk),
        out_shape=jax.ShapeDtypeStruct((g, k, n), BF16),
        compiler_params=_params("parallel"),
        name="s5_conv",
    )(at, wt, krep, vre, vim, ar, ai)


def _s5_weights(lam_re, lam_im, log_dt, b_re, b_im, c_re, c_im, d_skip):
    t = S5_CHUNK
    lr = jnp.minimum(lam_re.astype(F32), -1e-4)
    li = lam_im.astype(F32)
    dt = jnp.exp(log_dt.astype(F32))[:, None]
    mag = jnp.exp(lr * dt)
    a_re = mag * jnp.cos(li * dt)
    a_im = mag * jnp.sin(li * dt)
    den = lr * lr + li * li
    nr = a_re - 1.0
    f_re = (nr * lr + a_im * li) / den
    f_im = (a_im * lr - nr * li) / den
    br = b_re.astype(F32)
    bi = b_im.astype(F32)
    bb_re = f_re[..., None] * br - f_im[..., None] * bi
    bb_im = f_re[..., None] * bi + f_im[..., None] * br
    cr = c_re.astype(F32)
    ci = c_im.astype(F32)
    n_groups = cr.shape[0]

    def cpow(n):
        n = n.astype(F32)[:, None, None]
        m = jnp.exp(lr * dt * n)
        th = li * dt * n
        return m * jnp.cos(th), m * jnp.sin(th)

    pr, pi = cpow(jnp.arange(t + 1))
    car = cr[None] * pr[:, :, None, :] - ci[None] * pi[:, :, None, :]
    cai = cr[None] * pi[:, :, None, :] + ci[None] * pr[:, :, None, :]
    kk = (jnp.einsum('nghp,gpk->nghk', car[:t], bb_re) - jnp.einsum('nghp,gpk->nghk', cai[:t], bb_im))
    skip = d_skip.astype(F32).reshape(n_groups, GROUP_SIZE)
    kk = kk.at[0].add(skip[:, :, None] * jnp.eye(GROUP_SIZE, dtype=F32))
    krep = jnp.broadcast_to(kk.transpose(1, 0, 2, 3)[:, :, :, None, :], (n_groups, t, GROUP_SIZE, t, GROUP_SIZE))
    krep = krep.reshape(n_groups, t * GROUP_SIZE, t * GROUP_SIZE)
    qr, qi = pr[:t][::-1], pi[:t][::-1]
    w_re = jnp.einsum('sgp,gpk->gpsk', qr, bb_re) - jnp.einsum('sgp,gpk->gpsk', qi, bb_im)
    w_im = jnp.einsum('sgp,gpk->gpsk', qr, bb_im) + jnp.einsum('sgp,gpk->gpsk', qi, bb_re)
    wt = jnp.concatenate([w_re, w_im], axis=1).reshape(n_groups, 2 * STATE, t * GROUP_SIZE)
    v_re = car[1:].transpose(1, 0, 2, 3).reshape(n_groups, t * GROUP_SIZE, STATE)
    v_im = -cai[1:].transpose(1, 0, 2, 3).reshape(n_groups, t * GROUP_SIZE, STATE)
    first = (jnp.arange(n_groups) % S5_PAIR == 0)[:, None, None]

    def pack(v):
        z = jnp.zeros_like(v)
        return jnp.where(first, jnp.concatenate([v, z], axis=-1), jnp.concatenate([z, v], axis=-1)).astype(BF16)

    a_t = lambda v: v[t].reshape(n_groups // S5_PAIR, 1, S5_PAIR * STATE)
    return wt.astype(BF16), krep.astype(BF16), pack(v_re), pack(v_im), a_t(pr), a_t(pi)


def _tail(x, m, gpost, p, wpg, wpp):
    x1 = x + _rms(m, gpost)
    pg = _bdot(x1, wpg)
    pp = _bdot(p, wpp)
    return x1 + jax.nn.sigmoid(pg) * pp


def _s5_post_kernel(x_hbm, p_hbm, yt_ref, gate_ref, wglu_ref, bglu_ref, wout_ref, gpost_ref, wpg_ref, wpp_ref,
                    o_hbm, xbuf, pbuf, obuf, sem_x, sem_p, sem_o, *, n_steps, layer):
    k = pl.program_id(0)
    slot = _fetch_phase_rows(x_hbm, xbuf, sem_x, k, n_steps)
    _fetch_phase_rows(p_hbm, pbuf, sem_p, k, n_steps, lead=(layer,))
    n_groups = yt_ref.shape[0]
    yt = jnp.concatenate([yt_ref[g] for g in range(n_groups)], axis=0).astype(F32)
    g = jax.nn.gelu(yt).T
    z = _bdot(g, wglu_ref[...]) + bglu_ref[...]
    y = g * jax.nn.sigmoid(z)
    gate = gate_ref[...].astype(F32)
    y = y * (gate * jax.nn.sigmoid(gate))
    m = _bdot(y, wout_ref[...])
    res = _tail(xbuf[slot], m, gpost_ref[...], pbuf[slot], wpg_ref[...], wpp_ref[...])

    def out_copy(step, slot_):
        return pltpu.make_async_copy(obuf.at[slot_], _phase_rows(o_hbm, step), sem_o.at[slot_])

    @pl.when(k >= 2)
    def _():
        out_copy(k - 2, slot).wait()

    obuf[slot] = res
    out_copy(k, slot).start()

    @pl.when(k == n_steps - 1)
    def _():
        if n_steps >= 2:
            out_copy(k - 1, 1 - slot).wait()
        out_copy(k, slot).wait()


def _s5_post(x4, yt, gate, p5, layer, wglu, bglu, wout, gpost, wpg, wpp):
    bsz, nc, _, d = x4.shape
    pd = p5.shape[-1]
    n_groups = d // GROUP_SIZE
    n_steps = bsz * S5_CHUNK
    const = lambda k: (0, 0)
    return pl.pallas_call(
        functools.partial(_s5_post_kernel, n_steps=n_steps, layer=layer),
        grid=(n_steps,),
        in_specs=[
            pl.BlockSpec(memory_space=pl.ANY),
            pl.BlockSpec(memory_space=pl.ANY),
            pl.BlockSpec((n_groups, GROUP_SIZE, nc), lambda k: (0, k % S5_CHUNK, k // S5_CHUNK)),
            pl.BlockSpec((nc, d), lambda k: (k, 0)),
            pl.BlockSpec((d, d), const),
            pl.BlockSpec((1, d), const),
            pl.BlockSpec((d, d), const),
            pl.BlockSpec((1, d), const),
            pl.BlockSpec((d, d), const),
            pl.BlockSpec((pd, d), const),
        ],
        out_specs=pl.BlockSpec(memory_space=pl.ANY),
        out_shape=jax.ShapeDtypeStruct((bsz, nc, S5_CHUNK, d), F32),
        scratch_shapes=[pltpu.VMEM((2, nc, d), F32), pltpu.VMEM((2, nc, pd), F32), pltpu.VMEM((2, nc, d), F32),
                        pltpu.SemaphoreType.DMA((2,)), pltpu.SemaphoreType.DMA((2,)), pltpu.SemaphoreType.DMA((2,))],
        compiler_params=_params("arbitrary"),
        name="s5_post",
    )(x4, p5, yt, gate, wglu, bglu.reshape(1, d), wout, gpost.reshape(1, d), wpg, wpp)


def _attn_post_kernel(x_ref, og_ref, p_ref, wout_ref, gpost_ref, wpg_ref, wpp_ref, o_ref):
    m = jnp.dot(og_ref[...], wout_ref[...], preferred_element_type=F32)
    o_ref[...] = _tail(x_ref[...], m, gpost_ref[...], p_ref[...], wpg_ref[...], wpp_ref[...])


def _attn_post(x, og, p_all, layer, wout, gpost, wpg, wpp, *, tm):
    m, d = x.shape
    pd = p_all.shape[2]
    row = lambda i: (i, 0)
    const = lambda i: (0, 0)
    return pl.pallas_call(
        _attn_post_kernel,
        grid=(m // tm,),
        in_specs=[
            pl.BlockSpec((tm, d), row),
            pl.BlockSpec((tm, d), row),
            pl.BlockSpec((None, tm, pd), lambda i: (layer, i, 0)),
            pl.BlockSpec((d, d), const),
            pl.BlockSpec((1, d), const),
            pl.BlockSpec((d, d), const),
            pl.BlockSpec((pd, d), const),
        ],
        out_specs=pl.BlockSpec((tm, d), row),
        out_shape=jax.ShapeDtypeStruct((m, d), F32),
        compiler_params=_params("parallel"),
        name="attn_post",
    )(x, og, p_all, wout, gpost.reshape(1, d), wpg, wpp)


ATTN_HEADS = 4
ATTN_TQ = 128
ATTN_TQ_OUTER = 1024
LOG_F32_UNDERFLOW = 104.0
ATTN_PAIR = 2


def _sb_attn_kernel(q_ref, gate_ref, k_ref, v_ref, o_ref, acc_ref, c_ref):
    tq, nh = ATTN_TQ, ATTN_HEADS
    lanes = nh * HEAD_DIM
    rows = nh * tq
    n_inner = q_ref.shape[0] // tq
    io = pl.program_id(2)
    lane_head = lax.broadcasted_iota(jnp.int32, (tq, lanes), 1) // HEAD_DIM
    kr = lax.broadcasted_iota(jnp.int32, (tq, 2 * tq), 0)
    kc = lax.broadcasted_iota(jnp.int32, (tq, 2 * tq), 1)
    tri_ones = jnp.where((kr > kc) | (kc >= tq), 1.0, 0.0).astype(BF16)
    tw = 2 * tq
    tri_w = jnp.where(lax.broadcasted_iota(jnp.int32, (tw, tw), 0) > lax.broadcasted_iota(jnp.int32, (tw, tw), 1),
                      1.0, 0.0).astype(BF16)
    ones_w = jnp.ones((tw, tq), BF16)
    qrow =lax.broadcasted_iota(jnp.int32, (rows, tq), 0) % tq
    kcol = lax.broadcasted_iota(jnp.int32, (rows, tq), 1)
    causal = kcol < qrow
    scale = jnp.asarray(HEAD_DIM ** -0.5, BF16)

    def tile(q4, j, c_prev, mask):
        st = pl.multiple_of(j * tq, tq)
        kj = k_ref[pl.ds(st, tq), :]
        vj = v_ref[pl.ds(st, tq), :]
        z = lax.dot_general(q4, kj, (((1,), (1,)), ((), ())), preferred_element_type=F32)
        sp = jnp.maximum(z, 0.0) + jnp.log(1.0 + jnp.exp(-jnp.abs(z)))
        if mask is not None:
            sp = jnp.where(mask, sp, 0.0)
        both = jnp.dot(sp.astype(BF16), tri_ones, preferred_element_type=F32)
        tail, rs = both[:, :tq], both[:, tq:]
        logw = (z - sp) - tail
        if c_prev is not None:
            logw = logw - c_prev
        w = jnp.exp(logw)
        if mask is not None:
            w = jnp.where(mask, w, 0.0)
        pv = jnp.dot(w.astype(BF16), vj, preferred_element_type=F32)
        return pv, rs

    def window(q4, st, c_prev):
        kw = k_ref[pl.ds(st, tw), :]
        vw = v_ref[pl.ds(st, tw), :]
        z = lax.dot_general(q4, kw, (((1,), (1,)), ((), ())), preferred_element_type=F32)
        sp = jnp.maximum(z, 0.0) + jnp.log(1.0 + jnp.exp(-jnp.abs(z)))
        spb = sp.astype(BF16)
        tail = jnp.dot(spb, tri_w, preferred_element_type=F32)
        rs = jnp.dot(spb, ones_w, preferred_element_type=F32)
        w = jnp.exp((z - sp) - tail - jnp.concatenate([c_prev, c_prev], axis=1))
        pv = jnp.dot(w.astype(BF16), vw, preferred_element_type=F32)
        return pv, rs

    def q_pair(qp, _):
        i0 = io * n_inner + ATTN_PAIR * qp
        q4s, mins = [], []
        for s in range(ATTN_PAIR):
            r0 = pl.multiple_of((ATTN_PAIR * qp + s) * tq, tq)
            q = q_ref[pl.ds(r0, tq), :] * scale
            q4 = jnp.concatenate([jnp.where(lane_head == h, q, jnp.zeros_like(q)) for h in range(nh)], axis=0)
            pv, rs = tile(q4, i0 + s, None, causal)
            acc_ref[s] = pv
            c_ref[s] = rs
            q4s.append(q4)
            mins.append(jnp.min(rs))

        n_windows = i0 // 2

        def cond(state):
            n = state[0]
            unfinished = functools.reduce(jnp.logical_or, [m < LOG_F32_UNDERFLOW for m in state[1:]])
            return (n < n_windows) & unfinished

        def body(state):
            n = state[0]
            new_mins = []
            for s in range(ATTN_PAIR):
                st = pl.multiple_of((i0 + s) * tq - (n + 1) * tw, tq)
                c_prev = c_ref[s]
                pv, rs = window(q4s[s], st, c_prev)
                acc_ref[s] += pv
                c_new = c_prev + rs
                c_ref[s] = c_new
                new_mins.append(jnp.min(c_new))
            return (n + 1, *new_mins)

        final = lax.while_loop(cond, body, (jnp.int32(0), *mins))

        @pl.when((final[0] == n_windows) & (final[2] < LOG_F32_UNDERFLOW))
        def _():
            pv, _ = tile(q4s[1], 0, c_ref[1], None)
            acc_ref[1] += pv

        for s in range(ATTN_PAIR):
            r0 = pl.multiple_of((ATTN_PAIR * qp + s) * tq, tq)
            o = jnp.zeros((tq, lanes), F32)
            for h in range(nh):
                o = jnp.where(lane_head == h, acc_ref[s, h * tq:(h + 1) * tq, :], o)
            gate = gate_ref[pl.ds(r0, tq), :].astype(F32)
            o_ref[pl.ds(r0, tq), :] = (o * (gate * jax.nn.sigmoid(gate))).astype(o_ref.dtype)
        return 0

    lax.fori_loop(0, n_inner // ATTN_PAIR, q_pair, 0)


def _sb_attn(qg, kv):
    b, l, d2 = qg.shape
    d = d2 // 2
    lanes = ATTN_HEADS * HEAD_DIM
    nhg = d // lanes
    tqo = min(ATTN_TQ_OUTER, l)
    return pl.pallas_call(
        _sb_attn_kernel,
        grid=(b, nhg, l // tqo),
        in_specs=[
            pl.BlockSpec((None, tqo, lanes), lambda bi, hg, i: (bi, i, hg)),
            pl.BlockSpec((None, tqo, lanes), lambda bi, hg, i: (bi, i, nhg + hg)),
            pl.BlockSpec((None, l, lanes), lambda bi, hg, i: (bi, 0, hg)),
            pl.BlockSpec((None, l, lanes), lambda bi, hg, i: (bi, 0, nhg + hg)),
        ],
        out_specs=pl.BlockSpec((None, tqo, lanes), lambda bi, hg, i: (bi, i, hg)),
        out_shape=jax.ShapeDtypeStruct((b, l, d), BF16),
        scratch_shapes=[pltpu.VMEM((ATTN_PAIR, ATTN_HEADS * ATTN_TQ, lanes), F32),
                        pltpu.VMEM((ATTN_PAIR, ATTN_HEADS * ATTN_TQ, ATTN_TQ), F32)],
        compiler_params=_params("parallel", "parallel", "arbitrary"),
        name="sb_attn",
    )(qg, qg, kv, kv)


def kernel(x, p, a_norm_pre, a_norm_post, a_w_in, a_lam_re, a_lam_im, a_log_dt, a_b_re, a_b_im, a_c_re, a_c_im, a_d_skip, a_w_glu, a_b_glu, a_w_out, kv_norm, w_kv, b_norm_pre, b_norm_post, b_w_in, b_w_out, ple_w_proj, ple_w_gate):
    bsz, seq, d = x.shape
    depth = p.shape[0]
    n_a = a_w_in.shape[0]
    tokens = bsz * seq
    nc = seq // S5_CHUNK
    xf = x.reshape(tokens, d)
    p_all = p.reshape(depth, tokens, -1)
    k_v = None
    for i in range(depth):
        wpg = ple_w_gate[i].astype(BF16)
        wpp = ple_w_proj[i].astype(BF16)
        if i < n_a:
            j = i
            x4 = xf.reshape(bsz, nc, S5_CHUNK, d)
            p5 = p.reshape(depth, bsz, nc, S5_CHUNK, -1)
            wut = a_w_in[j][:, :d].T.astype(BF16)
            wg = a_w_in[j][:, d:].astype(BF16)
            ut, gate = _s5_in_proj(x4, a_norm_pre[j], wut, wg)
            s5_ops = _s5_weights(a_lam_re[j], a_lam_im[j], a_log_dt[j], a_b_re[j], a_b_im[j],
                                 a_c_re[j], a_c_im[j], a_d_skip[j])
            yt = _s5_conv(ut, *s5_ops, nc=nc)
            xf = _s5_post(x4, yt, gate, p5, i, a_w_glu[j].astype(BF16), a_b_glu[j], a_w_out[j].astype(BF16),
                          a_norm_post[j], wpg, wpp).reshape(tokens, d)
        else:
            j = i - n_a
            qg = _norm_matmul(xf, b_norm_pre[j], b_w_in[j].astype(BF16), tm=PROJ_ROWS, name="attn_in_proj")
            og = _sb_attn(qg.reshape(bsz, seq, 2 * d), k_v)
            xf = _attn_post(xf, og.reshape(tokens, d), p_all, i, b_w_out[j].astype(BF16), b_norm_post[j], wpg, wpp,
                            tm=POST_ROWS)
        if i == n_a - 1:
            k_v = _norm_matmul(xf, kv_norm, w_kv.astype(BF16), tm=PROJ_ROWS, name="kv_proj").reshape(bsz, seq, 2 * d)
    return xf.reshape(bsz, seq, d)
```

```python
import functools

import jax
import jax.numpy as jnp
from jax import lax
from jax.experimental import pallas as pl
from jax.experimental.pallas import tpu as pltpu

F32 = jnp.float32
BF16 = jnp.bfloat16

EPS = 1e-6
GROUP_SIZE = 16
STATE = 64
HEAD_DIM = 64
S5_CHUNK = 16
VMEM_LIMIT = 56 * 1024 * 1024
PROJ_ROWS = 1024
POST_ROWS = 512


def _params(*sem):
    return pltpu.CompilerParams(dimension_semantics=sem, vmem_limit_bytes=VMEM_LIMIT)


def _rms(x, g):
    ms = jnp.mean(x * x, axis=-1, keepdims=True)
    return x * lax.rsqrt(ms + EPS) * g


def _bdot(a, b):
    return jnp.dot(a.astype(BF16), b, preferred_element_type=F32)


def _norm_matmul_kernel(x_ref, g_ref, w_ref, o_ref):
    h = _rms(x_ref[...], g_ref[...])
    o_ref[...] = _bdot(h, w_ref[...]).astype(o_ref.dtype)


def _norm_matmul(x, g, w, *, tm, name):
    m, d = x.shape
    n = w.shape[1]
    return pl.pallas_call(
        _norm_matmul_kernel,
        grid=(m // tm,),
        in_specs=[
            pl.BlockSpec((tm, d), lambda i: (i, 0)),
            pl.BlockSpec((1, d), lambda i: (0, 0)),
            pl.BlockSpec((d, n), lambda i: (0, 0)),
        ],
        out_specs=pl.BlockSpec((tm, n), lambda i: (i, 0)),
        out_shape=jax.ShapeDtypeStruct((m, n), BF16),
        compiler_params=_params("parallel"),
        name=name,
    )(x, g.reshape(1, d), w)


def _phase_rows(hbm, step, lead=()):
    return hbm.at[(*lead, step // S5_CHUNK, slice(None), step % S5_CHUNK, slice(None))]


def _fetch_phase_rows(hbm, buf, sem, k, n_steps, lead=()):
    slot = k % 2

    def copy(step, slot_):
        return pltpu.make_async_copy(_phase_rows(hbm, step, lead), buf.at[slot_], sem.at[slot_])

    @pl.when(k == 0)
    def _():
        copy(k, slot).start()

    @pl.when(k + 1 < n_steps)
    def _():
        copy(k + 1, 1 - slot).start()

    copy(k, slot).wait()
    return slot


def _s5_in_proj_kernel(x_hbm, g_ref, wut_ref, wg_ref, ut_ref, gate_ref, xbuf, sem, *, n_steps):
    slot = _fetch_phase_rows(x_hbm, xbuf, sem, pl.program_id(0), n_steps)
    hn = _rms(xbuf[slot], g_ref[...]).astype(BF16)
    ut = lax.dot_general(wut_ref[...], hn, (((1,), (1,)), ((), ())), preferred_element_type=F32)
    n_groups = ut_ref.shape[0]
    ut = ut.astype(BF16)
    for g in range(n_groups):
        ut_ref[g] = ut[g * GROUP_SIZE:(g + 1) * GROUP_SIZE, :]
    gate_ref[...] = jnp.dot(hn, wg_ref[...], preferred_element_type=F32).astype(gate_ref.dtype)


def _s5_in_proj(x4, g, wut, wg):
    bsz, nc, _, d = x4.shape
    n_groups = d // GROUP_SIZE
    n_steps = bsz * S5_CHUNK
    const = lambda k: (0, 0)
    return pl.pallas_call(
        functools.partial(_s5_in_proj_kernel, n_steps=n_steps),
        grid=(n_steps,),
        in_specs=[
            pl.BlockSpec(memory_space=pl.ANY),
            pl.BlockSpec((1, d), const),
            pl.BlockSpec((d, d), const),
            pl.BlockSpec((d, d), const),
        ],
        out_specs=[
            pl.BlockSpec((n_groups, GROUP_SIZE, nc), lambda k: (0, k % S5_CHUNK, k // S5_CHUNK)),
            pl.BlockSpec((nc, d), lambda k: (k, 0)),
        ],
        out_shape=[
            jax.ShapeDtypeStruct((n_groups, S5_CHUNK * GROUP_SIZE, bsz * nc), BF16),
            jax.ShapeDtypeStruct((n_steps * nc, d), BF16),
        ],
        scratch_shapes=[pltpu.VMEM((2, nc, d), F32), pltpu.SemaphoreType.DMA((2,))],
        compiler_params=_params("arbitrary"),
        name="s5_in_proj",
    )(x4, g.reshape(1, d), wut, wg)


S5_PAIR = 2


def _s5_conv_kernel(at_ref, wt_ref, kr_ref, vre_ref, vim_ref, ar_ref, ai_ref, yt_ref, *, nc):
    levels = max(1, (nc - 1).bit_length())
    kdim = at_ref.shape[1]
    s = [jnp.dot(wt_ref[q], at_ref[q], preferred_element_type=F32) for q in range(S5_PAIR)]
    hr = jnp.concatenate([sq[:STATE] for sq in s], axis=0).T
    hi = jnp.concatenate([sq[STATE:] for sq in s], axis=0).T
    seg = lax.broadcasted_iota(jnp.int32, hr.shape, 0) % nc
    pr, pi = ar_ref[...], ai_ref[...]
    for k in range(levels):
        d = 1 << k
        keep = seg >= d
        sr = jnp.where(keep, pltpu.roll(hr, d, axis=0), 0.0)
        si = jnp.where(keep, pltpu.roll(hi, d, axis=0), 0.0)
        hr, hi = hr + pr * sr - pi * si, hi + pr * si + pi * sr
        pr, pi = pr * pr - pi * pi, 2.0 * pr * pi
    keep = seg >= 1
    hr = jnp.where(keep, pltpu.roll(hr, 1, axis=0), 0.0).astype(BF16)
    hi = jnp.where(keep, pltpu.roll(hi, 1, axis=0), 0.0).astype(BF16)
    lane_s = lax.broadcasted_iota(jnp.int32, (kdim, kdim), 1) // GROUP_SIZE
    nt = (((1,), (1,)), ((), ()))
    for q in range(S5_PAIR):
        krep = kr_ref[q]
        mt = jnp.zeros_like(krep)
        for sblk in range(S5_CHUNK):
            r = sblk * GROUP_SIZE
            shifted = krep if r == 0 else jnp.concatenate([jnp.zeros((r, kdim), krep.dtype), krep[:kdim - r]], axis=0)
            mt = jnp.where(lane_s == sblk, shifted, mt)
        y = jnp.dot(mt, at_ref[q], preferred_element_type=F32)
        y = y + lax.dot_general(vre_ref[q], hr, nt, preferred_element_type=F32)
        y = y + lax.dot_general(vim_ref[q], hi, nt, preferred_element_type=F32)
        yt_ref[q] = y.astype(yt_ref.dtype)


def _s5_conv(at, wt, krep, vre, vim, ar, ai, *, nc):
    g, k, n = at.shape
    st2 = 2 * STATE
    gs = S5_PAIR
    blk = lambda i: (i, 0, 0)
    return pl.pallas_call(
        functools.partial(_s5_conv_kernel, nc=nc),
        grid=(g // gs,),
        in_specs=[
            pl.BlockSpec((gs, k, n), blk),
            pl.BlockSpec((gs, st2, k), blk),
            pl.BlockSpec((gs, k, k), blk),
            pl.BlockSpec((gs, k, st2), blk),
            pl.BlockSpec((gs, k, st2), blk),
            pl.BlockSpec((None, 1, st2), blk),
            pl.BlockSpec((None, 1, st2), blk),
        ],
        out_specs=pl.BlockSpec((gs, k, n), blk),
        out_shape=jax.ShapeDtypeStruct((g, k, n), BF16),
        compiler_params=_params("parallel"),
        name="s5_conv",
    )(at, wt, krep, vre, vim, ar, ai)


def _s5_weights(lam_re, lam_im, log_dt, b_re, b_im, c_re, c_im, d_skip):
    t = S5_CHUNK
    lr = jnp.minimum(lam_re.astype(F32), -1e-4)
    li = lam_im.astype(F32)
    dt = jnp.exp(log_dt.astype(F32))[:, None]
    mag = jnp.exp(lr * dt)
    a_re = mag * jnp.cos(li * dt)
    a_im = mag * jnp.sin(li * dt)
    den = lr * lr + li * li
    nr = a_re - 1.0
    f_re = (nr * lr + a_im * li) / den
    f_im = (a_im * lr - nr * li) / den
    br = b_re.astype(F32)
    bi = b_im.astype(F32)
    bb_re = f_re[..., None] * br - f_im[..., None] * bi
    bb_im = f_re[..., None] * bi + f_im[..., None] * br
    cr = c_re.astype(F32)
    ci = c_im.astype(F32)
    n_groups = cr.shape[0]

    def cpow(n):
        n = n.astype(F32)[:, None, None]
        m = jnp.exp(lr * dt * n)
        th = li * dt * n
        return m * jnp.cos(th), m * jnp.sin(th)

    pr, pi = cpow(jnp.arange(t + 1))
    car = cr[None] * pr[:, :, None, :] - ci[None] * pi[:, :, None, :]
    cai = cr[None] * pi[:, :, None, :] + ci[None] * pr[:, :, None, :]
    kk = (jnp.einsum('nghp,gpk->nghk', car[:t], bb_re) - jnp.einsum('nghp,gpk->nghk', cai[:t], bb_im))
    skip = d_skip.astype(F32).reshape(n_groups, GROUP_SIZE)
    kk = kk.at[0].add(skip[:, :, None] * jnp.eye(GROUP_SIZE, dtype=F32))
    krep = jnp.broadcast_to(kk.transpose(1, 0, 2, 3)[:, :, :, None, :], (n_groups, t, GROUP_SIZE, t, GROUP_SIZE))
    krep = krep.reshape(n_groups, t * GROUP_SIZE, t * GROUP_SIZE)
    qr, qi = pr[:t][::-1], pi[:t][::-1]
    w_re = jnp.einsum('sgp,gpk->gpsk', qr, bb_re) - jnp.einsum('sgp,gpk->gpsk', qi, bb_im)
    w_im = jnp.einsum('sgp,gpk->gpsk', qr, bb_im) + jnp.einsum('sgp,gpk->gpsk', qi, bb_re)
    wt = jnp.concatenate([w_re, w_im], axis=1).reshape(n_groups, 2 * STATE, t * GROUP_SIZE)
    v_re = car[1:].transpose(1, 0, 2, 3).reshape(n_groups, t * GROUP_SIZE, STATE)
    v_im = -cai[1:].transpose(1, 0, 2, 3).reshape(n_groups, t * GROUP_SIZE, STATE)
    first = (jnp.arange(n_groups) % S5_PAIR == 0)[:, None, None]

    def pack(v):
        z = jnp.zeros_like(v)
        return jnp.where(first, jnp.concatenate([v, z], axis=-1), jnp.concatenate([z, v], axis=-1)).astype(BF16)

    a_t = lambda v: v[t].reshape(n_groups // S5_PAIR, 1, S5_PAIR * STATE)
    return wt.astype(BF16), krep.astype(BF16), pack(v_re), pack(v_im), a_t(pr), a_t(pi)


def _tail(x, m, gpost, p, wpg, wpp):
    x1 = x + _rms(m, gpost)
    pg = _bdot(x1, wpg)
    pp = _bdot(p, wpp)
    return x1 + jax.nn.sigmoid(pg) * pp


def _s5_post_kernel(x_hbm, p_hbm, yt_ref, gate_ref, wglu_ref, bglu_ref, wout_ref, gpost_ref, wpg_ref, wpp_ref,
                    o_hbm, xbuf, pbuf, obuf, sem_x, sem_p, sem_o, *, n_steps, layer):
    k = pl.program_id(0)
    slot = _fetch_phase_rows(x_hbm, xbuf, sem_x, k, n_steps)
    _fetch_phase_rows(p_hbm, pbuf, sem_p, k, n_steps, lead=(layer,))
    n_groups = yt_ref.shape[0]
    yt = jnp.concatenate([yt_ref[g] for g in range(n_groups)], axis=0).astype(F32)
    g = jax.nn.gelu(yt).T
    z = _bdot(g, wglu_ref[...]) + bglu_ref[...]
    y = g * jax.nn.sigmoid(z)
    gate = gate_ref[...].astype(F32)
    y = y * (gate * jax.nn.sigmoid(gate))
    m = _bdot(y, wout_ref[...])
    res = _tail(xbuf[slot], m, gpost_ref[...], pbuf[slot], wpg_ref[...], wpp_ref[...])

    def out_copy(step, slot_):
        return pltpu.make_async_copy(obuf.at[slot_], _phase_rows(o_hbm, step), sem_o.at[slot_])

    @pl.when(k >= 2)
    def _():
        out_copy(k - 2, slot).wait()

    obuf[slot] = res
    out_copy(k, slot).start()

    @pl.when(k == n_steps - 1)
    def _():
        if n_steps >= 2:
            out_copy(k - 1, 1 - slot).wait()
        out_copy(k, slot).wait()


def _s5_post(x4, yt, gate, p5, layer, wglu, bglu, wout, gpost, wpg, wpp):
    bsz, nc, _, d = x4.shape
    pd = p5.shape[-1]
    n_groups = d // GROUP_SIZE
    n_steps = bsz * S5_CHUNK
    const = lambda k: (0, 0)
    return pl.pallas_call(
        functools.partial(_s5_post_kernel, n_steps=n_steps, layer=layer),
        grid=(n_steps,),
        in_specs=[
            pl.BlockSpec(memory_space=pl.ANY),
            pl.BlockSpec(memory_space=pl.ANY),
            pl.BlockSpec((n_groups, GROUP_SIZE, nc), lambda k: (0, k % S5_CHUNK, k // S5_CHUNK)),
            pl.BlockSpec((nc, d), lambda k: (k, 0)),
            pl.BlockSpec((d, d), const),
            pl.BlockSpec((1, d), const),
            pl.BlockSpec((d, d), const),
            pl.BlockSpec((1, d), const),
            pl.BlockSpec((d, d), const),
            pl.BlockSpec((pd, d), const),
        ],
        out_specs=pl.BlockSpec(memory_space=pl.ANY),
        out_shape=jax.ShapeDtypeStruct((bsz, nc, S5_CHUNK, d), F32),
        scratch_shapes=[pltpu.VMEM((2, nc, d), F32), pltpu.VMEM((2, nc, pd), F32), pltpu.VMEM((2, nc, d), F32),
                        pltpu.SemaphoreType.DMA((2,)), pltpu.SemaphoreType.DMA((2,)), pltpu.SemaphoreType.DMA((2,))],
        compiler_params=_params("arbitrary"),
        name="s5_post",
    )(x4, p5, yt, gate, wglu, bglu.reshape(1, d), wout, gpost.reshape(1, d), wpg, wpp)


def _attn_post_kernel(x_ref, og_ref, p_ref, wout_ref, gpost_ref, wpg_ref, wpp_ref, o_ref):
    m = jnp.dot(og_ref[...], wout_ref[...], preferred_element_type=F32)
    o_ref[...] = _tail(x_ref[...], m, gpost_ref[...], p_ref[...], wpg_ref[...], wpp_ref[...])


def _attn_post(x, og, p_all, layer, wout, gpost, wpg, wpp, *, tm):
    m, d = x.shape
    pd = p_all.shape[2]
    row = lambda i: (i, 0)
    const = lambda i: (0, 0)
    return pl.pallas_call(
        _attn_post_kernel,
        grid=(m // tm,),
        in_specs=[
            pl.BlockSpec((tm, d), row),
            pl.BlockSpec((tm, d), row),
            pl.BlockSpec((None, tm, pd), lambda i: (layer, i, 0)),
            pl.BlockSpec((d, d), const),
            pl.BlockSpec((1, d), const),
            pl.BlockSpec((d, d), const),
            pl.BlockSpec((pd, d), const),
        ],
        out_specs=pl.BlockSpec((tm, d), row),
        out_shape=jax.ShapeDtypeStruct((m, d), F32),
        compiler_params=_params("parallel"),
        name="attn_post",
    )(x, og, p_all, wout, gpost.reshape(1, d), wpg, wpp)


ATTN_HEADS = 4
ATTN_TQ = 128
ATTN_TQ_OUTER = 1024
LOG_F32_UNDERFLOW = 104.0
ATTN_PAIR = 2
NO_WINDOW = 1e30


def _sb_attn_kernel(q_ref, gate_ref, k_ref, v_ref, o_ref, acc_ref, c_ref):
    tq, nh = ATTN_TQ, ATTN_HEADS
    lanes = nh * HEAD_DIM
    rows = nh * tq
    n_inner = q_ref.shape[0] // tq
    io = pl.program_id(2)
    lane_head = lax.broadcasted_iota(jnp.int32, (tq, lanes), 1) // HEAD_DIM
    kr = lax.broadcasted_iota(jnp.int32, (tq, 2 * tq), 0)
    kc = lax.broadcasted_iota(jnp.int32, (tq, 2 * tq), 1)
    tri_ones = jnp.where((kr > kc) | (kc >= tq), 1.0, 0.0).astype(BF16)
    tw = 2 * tq
    tri_w = jnp.where(lax.broadcasted_iota(jnp.int32, (tw, tw), 0) > lax.broadcasted_iota(jnp.int32, (tw, tw), 1),
                      1.0, 0.0).astype(BF16)
    ones_w = jnp.ones((tw, tq), BF16)
    qrow =lax.broadcasted_iota(jnp.int32, (rows, tq), 0) % tq
    kcol = lax.broadcasted_iota(jnp.int32, (rows, tq), 1)
    causal = kcol < qrow
    scale = jnp.asarray(HEAD_DIM ** -0.5, BF16)

    def tile(q4, j, c_prev, mask):
        st = pl.multiple_of(j * tq, tq)
        kj = k_ref[pl.ds(st, tq), :]
        vj = v_ref[pl.ds(st, tq), :]
        z = lax.dot_general(q4, kj, (((1,), (1,)), ((), ())), preferred_element_type=F32)
        sp = jnp.maximum(z, 0.0) + jnp.log(1.0 + jnp.exp(-jnp.abs(z)))
        if mask is not None:
            sp = jnp.where(mask, sp, 0.0)
        both = jnp.dot(sp.astype(BF16), tri_ones, preferred_element_type=F32)
        tail, rs = both[:, :tq], both[:, tq:]
        logw = (z - sp) - tail
        if c_prev is not None:
            logw = logw - c_prev
        w = jnp.exp(logw)
        if mask is not None:
            w = jnp.where(mask, w, 0.0)
        pv = jnp.dot(w.astype(BF16), vj, preferred_element_type=F32)
        return pv, rs

    def window(q4, st, c_prev):
        kw = k_ref[pl.ds(st, tw), :]
        vw = v_ref[pl.ds(st, tw), :]
        z = lax.dot_general(q4, kw, (((1,), (1,)), ((), ())), preferred_element_type=F32)
        sp = jnp.maximum(z, 0.0) + jnp.log(1.0 + jnp.exp(-jnp.abs(z)))
        spb = sp.astype(BF16)
        tail = jnp.dot(spb, tri_w, preferred_element_type=F32)
        rs = jnp.dot(spb, ones_w, preferred_element_type=F32)
        w = jnp.exp((z - sp) - tail - jnp.concatenate([c_prev, c_prev], axis=1))
        pv = jnp.dot(w.astype(BF16), vw, preferred_element_type=F32)
        return pv, rs

    def q_pair(qp, _):
        i0 = io * n_inner + ATTN_PAIR * qp
        n_windows = i0 // 2
        has_window = n_windows > 0
        q4s, mins = [], []
        for s in range(ATTN_PAIR):
            r0 = pl.multiple_of((ATTN_PAIR * qp + s) * tq, tq)
            q = q_ref[pl.ds(r0, tq), :] * scale
            q4s.append(jnp.concatenate([jnp.where(lane_head == h, q, jnp.zeros_like(q)) for h in range(nh)], axis=0))
        for s in range(ATTN_PAIR):
            pv_d, rs_d = tile(q4s[s], i0 + s, None, causal)
            st = pl.multiple_of(jnp.maximum((i0 + s) * tq - tw, 0), tq)
            pv_w, rs_w = window(q4s[s], st, jnp.where(has_window, rs_d, NO_WINDOW))
            c = rs_d + jnp.where(has_window, rs_w, 0.0)
            acc_ref[s] = pv_d + pv_w
            c_ref[s] = c
            mins.append(jnp.min(c))

        def cond(state):
            n = state[0]
            unfinished = functools.reduce(jnp.logical_or, [m < LOG_F32_UNDERFLOW for m in state[1:]])
            return (n < n_windows) & unfinished

        def body(state):
            n = state[0]
            new_mins = []
            for s in range(ATTN_PAIR):
                st = pl.multiple_of((i0 + s) * tq - (n + 1) * tw, tq)
                c_prev = c_ref[s]
                pv, rs = window(q4s[s], st, c_prev)
                acc_ref[s] += pv
                c_new = c_prev + rs
                c_ref[s] = c_new
                new_mins.append(jnp.min(c_new))
            return (n + 1, *new_mins)

        final = lax.while_loop(cond, body, (jnp.int32(1), *mins))

        @pl.when((final[0] >= n_windows) & (final[2] < LOG_F32_UNDERFLOW))
        def _():
            pv, _ = tile(q4s[1], 0, c_ref[1], None)
            acc_ref[1] += pv

        for s in range(ATTN_PAIR):
            r0 = pl.multiple_of((ATTN_PAIR * qp + s) * tq, tq)
            o = jnp.zeros((tq, lanes), F32)
            for h in range(nh):
                o = jnp.where(lane_head == h, acc_ref[s, h * tq:(h + 1) * tq, :], o)
            gate = gate_ref[pl.ds(r0, tq), :].astype(F32)
            o_ref[pl.ds(r0, tq), :] = (o * (gate * jax.nn.sigmoid(gate))).astype(o_ref.dtype)
        return 0

    lax.fori_loop(0, n_inner // ATTN_PAIR, q_pair, 0)


def _sb_attn(qg, kv):
    b, l, d2 = qg.shape
    d = d2 // 2
    lanes = ATTN_HEADS * HEAD_DIM
    nhg = d // lanes
    tqo = min(ATTN_TQ_OUTER, l)
    return pl.pallas_call(
        _sb_attn_kernel,
        grid=(b, nhg, l // tqo),
        in_specs=[
            pl.BlockSpec((None, tqo, lanes), lambda bi, hg, i: (bi, i, hg)),
            pl.BlockSpec((None, tqo, lanes), lambda bi, hg, i: (bi, i, nhg + hg)),
            pl.BlockSpec((None, l, lanes), lambda bi, hg, i: (bi, 0, hg)),
            pl.BlockSpec((None, l, lanes), lambda bi, hg, i: (bi, 0, nhg + hg)),
        ],
        out_specs=pl.BlockSpec((None, tqo, lanes), lambda bi, hg, i: (bi, i, hg)),
        out_shape=jax.ShapeDtypeStruct((b, l, d), BF16),
        scratch_shapes=[pltpu.VMEM((ATTN_PAIR, ATTN_HEADS * ATTN_TQ, lanes), F32),
                        pltpu.VMEM((ATTN_PAIR, ATTN_HEADS * ATTN_TQ, ATTN_TQ), F32)],
        compiler_params=_params("parallel", "parallel", "arbitrary"),
        name="sb_attn",
    )(qg, qg, kv, kv)


def kernel(x, p, a_norm_pre, a_norm_post, a_w_in, a_lam_re, a_lam_im, a_log_dt, a_b_re, a_b_im, a_c_re, a_c_im, a_d_skip, a_w_glu, a_b_glu, a_w_out, kv_norm, w_kv, b_norm_pre, b_norm_post, b_w_in, b_w_out, ple_w_proj, ple_w_gate):
    bsz, seq, d = x.shape
    depth = p.shape[0]
    n_a = a_w_in.shape[0]
    tokens = bsz * seq
    nc = seq // S5_CHUNK
    xf = x.reshape(tokens, d)
    p_all = p.reshape(depth, tokens, -1)
    k_v = None
    for i in range(depth):
        wpg = ple_w_gate[i].astype(BF16)
        wpp = ple_w_proj[i].astype(BF16)
        if i < n_a:
            j = i
            x4 = xf.reshape(bsz, nc, S5_CHUNK, d)
            p5 = p.reshape(depth, bsz, nc, S5_CHUNK, -1)
            wut = a_w_in[j][:, :d].T.astype(BF16)
            wg = a_w_in[j][:, d:].astype(BF16)
            ut, gate = _s5_in_proj(x4, a_norm_pre[j], wut, wg)
            s5_ops = _s5_weights(a_lam_re[j], a_lam_im[j], a_log_dt[j], a_b_re[j], a_b_im[j],
                                 a_c_re[j], a_c_im[j], a_d_skip[j])
            yt = _s5_conv(ut, *s5_ops, nc=nc)
            xf = _s5_post(x4, yt, gate, p5, i, a_w_glu[j].astype(BF16), a_b_glu[j], a_w_out[j].astype(BF16),
                          a_norm_post[j], wpg, wpp).reshape(tokens, d)
        else:
            j = i - n_a
            qg = _norm_matmul(xf, b_norm_pre[j], b_w_in[j].astype(BF16), tm=PROJ_ROWS, name="attn_in_proj")
            og = _sb_attn(qg.reshape(bsz, seq, 2 * d), k_v)
            xf = _attn_post(xf, og.reshape(tokens, d), p_all, i, b_w_out[j].astype(BF16), b_norm_post[j], wpg, wpp,
                            tm=POST_ROWS)
        if i == n_a - 1:
            k_v = _norm_matmul(xf, kv_norm, w_kv.astype(BF16), tm=PROJ_ROWS, name="kv_proj").reshape(bsz, seq, 2 * d)
    return xf.reshape(bsz, seq, d)
```

```python
import functools

import jax
import jax.numpy as jnp
from jax import lax
from jax.experimental import pallas as pl
from jax.experimental.pallas import tpu as pltpu

F32 = jnp.float32
BF16 = jnp.bfloat16

EPS = 1e-6
GROUP_SIZE = 16
STATE = 64
HEAD_DIM = 64
S5_CHUNK = 16
VMEM_LIMIT = 56 * 1024 * 1024
PROJ_ROWS = 1024
POST_ROWS = 512


def _params(*sem):
    return pltpu.CompilerParams(dimension_semantics=sem, vmem_limit_bytes=VMEM_LIMIT)


def _rms(x, g):
    ms = jnp.mean(x * x, axis=-1, keepdims=True)
    return x * lax.rsqrt(ms + EPS) * g


def _bdot(a, b):
    return jnp.dot(a.astype(BF16), b, preferred_element_type=F32)


def _norm_matmul_kernel(x_ref, g_ref, w_ref, o_ref):
    h = _rms(x_ref[...], g_ref[...])
    o_ref[...] = _bdot(h, w_ref[...]).astype(o_ref.dtype)


def _norm_matmul(x, g, w, *, tm, name):
    m, d = x.shape
    n = w.shape[1]
    return pl.pallas_call(
        _norm_matmul_kernel,
        grid=(m // tm,),
        in_specs=[
            pl.BlockSpec((tm, d), lambda i: (i, 0)),
            pl.BlockSpec((1, d), lambda i: (0, 0)),
            pl.BlockSpec((d, n), lambda i: (0, 0)),
        ],
        out_specs=pl.BlockSpec((tm, n), lambda i: (i, 0)),
        out_shape=jax.ShapeDtypeStruct((m, n), BF16),
        compiler_params=_params("parallel"),
        name=name,
    )(x, g.reshape(1, d), w)


def _phase_rows(hbm, step, lead=()):
    return hbm.at[(*lead, step // S5_CHUNK, slice(None), step % S5_CHUNK, slice(None))]


def _fetch_phase_rows(hbm, buf, sem, k, n_steps, lead=()):
    slot = k % 2

    def copy(step, slot_):
        return pltpu.make_async_copy(_phase_rows(hbm, step, lead), buf.at[slot_], sem.at[slot_])

    @pl.when(k == 0)
    def _():
        copy(k, slot).start()

    @pl.when(k + 1 < n_steps)
    def _():
        copy(k + 1, 1 - slot).start()

    copy(k, slot).wait()
    return slot


def _s5_in_proj_kernel(x_hbm, g_ref, wut_ref, wg_ref, ut_ref, gate_ref, xbuf, sem, *, n_steps):
    slot = _fetch_phase_rows(x_hbm, xbuf, sem, pl.program_id(0), n_steps)
    hn = _rms(xbuf[slot], g_ref[...]).astype(BF16)
    ut = lax.dot_general(wut_ref[...], hn, (((1,), (1,)), ((), ())), preferred_element_type=F32)
    n_groups = ut_ref.shape[0]
    ut = ut.astype(BF16)
    for g in range(n_groups):
        ut_ref[g] = ut[g * GROUP_SIZE:(g + 1) * GROUP_SIZE, :]
    gate_ref[...] = jnp.dot(hn, wg_ref[...], preferred_element_type=F32).astype(gate_ref.dtype)


def _s5_in_proj(x4, g, wut, wg):
    bsz, nc, _, d = x4.shape
    n_groups = d // GROUP_SIZE
    n_steps = bsz * S5_CHUNK
    const = lambda k: (0, 0)
    return pl.pallas_call(
        functools.partial(_s5_in_proj_kernel, n_steps=n_steps),
        grid=(n_steps,),
        in_specs=[
            pl.BlockSpec(memory_space=pl.ANY),
            pl.BlockSpec((1, d), const),
            pl.BlockSpec((d, d), const),
            pl.BlockSpec((d, d), const),
        ],
        out_specs=[
            pl.BlockSpec((n_groups, GROUP_SIZE, nc), lambda k: (0, k % S5_CHUNK, k // S5_CHUNK)),
            pl.BlockSpec((nc, d), lambda k: (k, 0)),
        ],
        out_shape=[
            jax.ShapeDtypeStruct((n_groups, S5_CHUNK * GROUP_SIZE, bsz * nc), BF16),
            jax.ShapeDtypeStruct((n_steps * nc, d), BF16),
        ],
        scratch_shapes=[pltpu.VMEM((2, nc, d), F32), pltpu.SemaphoreType.DMA((2,))],
        compiler_params=_params("arbitrary"),
        name="s5_in_proj",
    )(x4, g.reshape(1, d), wut, wg)


S5_PAIR = 2


def _s5_conv_kernel(at_ref, wt_ref, kr_ref, vre_ref, vim_ref, ar_ref, ai_ref, yt_ref, *, nc):
    levels = max(1, (nc - 1).bit_length())
    kdim = at_ref.shape[1]
    s = [jnp.dot(wt_ref[q], at_ref[q], preferred_element_type=F32) for q in range(S5_PAIR)]
    hr = jnp.concatenate([sq[:STATE] for sq in s], axis=0).T
    hi = jnp.concatenate([sq[STATE:] for sq in s], axis=0).T
    seg = lax.broadcasted_iota(jnp.int32, hr.shape, 0) % nc
    pr, pi = ar_ref[...], ai_ref[...]
    for k in range(levels):
        d = 1 << k
        keep = seg >= d
        sr = jnp.where(keep, pltpu.roll(hr, d, axis=0), 0.0)
        si = jnp.where(keep, pltpu.roll(hi, d, axis=0), 0.0)
        hr, hi = hr + pr * sr - pi * si, hi + pr * si + pi * sr
        pr, pi = pr * pr - pi * pi, 2.0 * pr * pi
    keep = seg >= 1
    hr = jnp.where(keep, pltpu.roll(hr, 1, axis=0), 0.0).astype(BF16)
    hi = jnp.where(keep, pltpu.roll(hi, 1, axis=0), 0.0).astype(BF16)
    lane_s = lax.broadcasted_iota(jnp.int32, (kdim, kdim), 1) // GROUP_SIZE
    rep = jnp.where(lax.broadcasted_iota(jnp.int32, (2 * STATE, kdim), 0)
                    == lax.broadcasted_iota(jnp.int32, (2 * STATE, kdim), 1) % GROUP_SIZE, 1.0, 0.0).astype(BF16)
    nt = (((1,), (1,)), ((), ()))
    for q in range(S5_PAIR):
        krep = jnp.dot(kr_ref[q], rep, preferred_element_type=F32).astype(BF16)
        mt = jnp.zeros_like(krep)
        for sblk in range(S5_CHUNK):
            r = sblk * GROUP_SIZE
            shifted = krep if r == 0 else jnp.concatenate([jnp.zeros((r, kdim), krep.dtype), krep[:kdim - r]], axis=0)
            mt = jnp.where(lane_s == sblk, shifted, mt)
        y = jnp.dot(mt, at_ref[q], preferred_element_type=F32)
        y = y + lax.dot_general(vre_ref[q], hr, nt, preferred_element_type=F32)
        y = y + lax.dot_general(vim_ref[q], hi, nt, preferred_element_type=F32)
        yt_ref[q] = y.astype(yt_ref.dtype)


def _s5_conv(at, wt, krep, vre, vim, ar, ai, *, nc):
    g, k, n = at.shape
    st2 = 2 * STATE
    gs = S5_PAIR
    blk = lambda i: (i, 0, 0)
    return pl.pallas_call(
        functools.partial(_s5_conv_kernel, nc=nc),
        grid=(g // gs,),
        in_specs=[
            pl.BlockSpec((gs, k, n), blk),
            pl.BlockSpec((gs, st2, k), blk),
            pl.BlockSpec((gs, k, st2), blk),
            pl.BlockSpec((gs, k, st2), blk),
            pl.BlockSpec((gs, k, st2), blk),
            pl.BlockSpec((None, 1, st2), blk),
            pl.BlockSpec((None, 1, st2), blk),
        ],
        out_specs=pl.BlockSpec((gs, k, n), blk),
        out_shape=jax.ShapeDtypeStruct((g, k, n), BF16),
        compiler_params=_params("parallel"),
        name="s5_conv",
    )(at, wt, krep, vre, vim, ar, ai)


def _s5_weights(lam_re, lam_im, log_dt, b_re, b_im, c_re, c_im, d_skip):
    t = S5_CHUNK
    lr = jnp.minimum(lam_re.astype(F32), -1e-4)
    li = lam_im.astype(F32)
    dt = jnp.exp(log_dt.astype(F32))[:, None]
    mag = jnp.exp(lr * dt)
    a_re = mag * jnp.cos(li * dt)
    a_im = mag * jnp.sin(li * dt)
    den = lr * lr + li * li
    nr = a_re - 1.0
    f_re = (nr * lr + a_im * li) / den
    f_im = (a_im * lr - nr * li) / den
    br = b_re.astype(F32)
    bi = b_im.astype(F32)
    bb_re = f_re[..., None] * br - f_im[..., None] * bi
    bb_im = f_re[..., None] * bi + f_im[..., None] * br
    cr = c_re.astype(F32)
    ci = c_im.astype(F32)
    n_groups = cr.shape[0]

    def cpow(n):
        n = n.astype(F32)[:, None, None]
        m = jnp.exp(lr * dt * n)
        th = li * dt * n
        return m * jnp.cos(th), m * jnp.sin(th)

    pr, pi = cpow(jnp.arange(t + 1))
    prg, pig = pr.transpose(1, 0, 2)[:, :, None, :], pi.transpose(1, 0, 2)[:, :, None, :]
    car = cr[:, None] * prg - ci[:, None] * pig
    cai = cr[:, None] * pig + ci[:, None] * prg
    ca = jnp.concatenate([car[:, :t], -cai[:, :t]], axis=-1).reshape(n_groups, t * GROUP_SIZE, 2 * STATE)
    bb = jnp.concatenate([bb_re, bb_im], axis=1)
    kst = jnp.einsum('gmp,gpk->gmk', ca, bb, precision=lax.Precision.HIGHEST)
    skip = d_skip.astype(F32).reshape(n_groups, GROUP_SIZE)
    kst = kst.at[:, :GROUP_SIZE, :].add(skip[:, :, None] * jnp.eye(GROUP_SIZE, dtype=F32))
    kst = jnp.pad(kst, ((0, 0), (0, 0), (0, 2 * STATE - GROUP_SIZE)))
    qr, qi = pr[:t][::-1], pi[:t][::-1]
    w_re = jnp.einsum('sgp,gpk->gpsk', qr, bb_re) - jnp.einsum('sgp,gpk->gpsk', qi, bb_im)
    w_im = jnp.einsum('sgp,gpk->gpsk', qr, bb_im) + jnp.einsum('sgp,gpk->gpsk', qi, bb_re)
    wt = jnp.concatenate([w_re, w_im], axis=1).reshape(n_groups, 2 * STATE, t * GROUP_SIZE)
    v_re = car[:, 1:].reshape(n_groups, t * GROUP_SIZE, STATE)
    v_im = -cai[:, 1:].reshape(n_groups, t * GROUP_SIZE, STATE)
    first = (jnp.arange(n_groups) % S5_PAIR == 0)[:, None, None]

    def pack(v):
        z = jnp.zeros_like(v)
        return jnp.where(first, jnp.concatenate([v, z], axis=-1), jnp.concatenate([z, v], axis=-1)).astype(BF16)

    a_t = lambda v: v[t].reshape(n_groups // S5_PAIR, 1, S5_PAIR * STATE)
    return wt.astype(BF16), kst.astype(BF16), pack(v_re), pack(v_im), a_t(pr), a_t(pi)


def _tail(x, m, gpost, p, wpg, wpp):
    x1 = x + _rms(m, gpost)
    pg = _bdot(x1, wpg)
    pp = _bdot(p, wpp)
    return x1 + jax.nn.sigmoid(pg) * pp


def _s5_post_kernel(x_hbm, p_hbm, yt_ref, gate_ref, wglu_ref, bglu_ref, wout_ref, gpost_ref, wpg_ref, wpp_ref,
                    o_hbm, xbuf, pbuf, obuf, sem_x, sem_p, sem_o, *, n_steps, layer):
    k = pl.program_id(0)
    slot = _fetch_phase_rows(x_hbm, xbuf, sem_x, k, n_steps)
    _fetch_phase_rows(p_hbm, pbuf, sem_p, k, n_steps, lead=(layer,))
    n_groups = yt_ref.shape[0]
    yt = jnp.concatenate([yt_ref[g] for g in range(n_groups)], axis=0).astype(F32)
    g = jax.nn.gelu(yt).T
    z = _bdot(g, wglu_ref[...]) + bglu_ref[...]
    y = g * jax.nn.sigmoid(z)
    gate = gate_ref[...].astype(F32)
    y = y * (gate * jax.nn.sigmoid(gate))
    m = _bdot(y, wout_ref[...])
    res = _tail(xbuf[slot], m, gpost_ref[...], pbuf[slot], wpg_ref[...], wpp_ref[...])

    def out_copy(step, slot_):
        return pltpu.make_async_copy(obuf.at[slot_], _phase_rows(o_hbm, step), sem_o.at[slot_])

    @pl.when(k >= 2)
    def _():
        out_copy(k - 2, slot).wait()

    obuf[slot] = res
    out_copy(k, slot).start()

    @pl.when(k == n_steps - 1)
    def _():
        if n_steps >= 2:
            out_copy(k - 1, 1 - slot).wait()
        out_copy(k, slot).wait()


def _s5_post(x4, yt, gate, p5, layer, wglu, bglu, wout, gpost, wpg, wpp):
    bsz, nc, _, d = x4.shape
    pd = p5.shape[-1]
    n_groups = d // GROUP_SIZE
    n_steps = bsz * S5_CHUNK
    const = lambda k: (0, 0)
    return pl.pallas_call(
        functools.partial(_s5_post_kernel, n_steps=n_steps, layer=layer),
        grid=(n_steps,),
        in_specs=[
            pl.BlockSpec(memory_space=pl.ANY),
            pl.BlockSpec(memory_space=pl.ANY),
            pl.BlockSpec((n_groups, GROUP_SIZE, nc), lambda k: (0, k % S5_CHUNK, k // S5_CHUNK)),
            pl.BlockSpec((nc, d), lambda k: (k, 0)),
            pl.BlockSpec((d, d), const),
            pl.BlockSpec((1, d), const),
            pl.BlockSpec((d, d), const),
            pl.BlockSpec((1, d), const),
            pl.BlockSpec((d, d), const),
            pl.BlockSpec((pd, d), const),
        ],
        out_specs=pl.BlockSpec(memory_space=pl.ANY),
        out_shape=jax.ShapeDtypeStruct((bsz, nc, S5_CHUNK, d), F32),
        scratch_shapes=[pltpu.VMEM((2, nc, d), F32), pltpu.VMEM((2, nc, pd), F32), pltpu.VMEM((2, nc, d), F32),
                        pltpu.SemaphoreType.DMA((2,)), pltpu.SemaphoreType.DMA((2,)), pltpu.SemaphoreType.DMA((2,))],
        compiler_params=_params("arbitrary"),
        name="s5_post",
    )(x4, p5, yt, gate, wglu, bglu.reshape(1, d), wout, gpost.reshape(1, d), wpg, wpp)


def _attn_post_kernel(x_ref, og_ref, p_ref, wout_ref, gpost_ref, wpg_ref, wpp_ref, o_ref):
    m = jnp.dot(og_ref[...], wout_ref[...], preferred_element_type=F32)
    o_ref[...] = _tail(x_ref[...], m, gpost_ref[...], p_ref[...], wpg_ref[...], wpp_ref[...])


def _attn_post(x, og, p_all, layer, wout, gpost, wpg, wpp, *, tm):
    m, d = x.shape
    pd = p_all.shape[2]
    row = lambda i: (i, 0)
    const = lambda i: (0, 0)
    return pl.pallas_call(
        _attn_post_kernel,
        grid=(m // tm,),
        in_specs=[
            pl.BlockSpec((tm, d), row),
            pl.BlockSpec((tm, d), row),
            pl.BlockSpec((None, tm, pd), lambda i: (layer, i, 0)),
            pl.BlockSpec((d, d), const),
            pl.BlockSpec((1, d), const),
            pl.BlockSpec((d, d), const),
            pl.BlockSpec((pd, d), const),
        ],
        out_specs=pl.BlockSpec((tm, d), row),
        out_shape=jax.ShapeDtypeStruct((m, d), F32),
        compiler_params=_params("parallel"),
        name="attn_post",
    )(x, og, p_all, wout, gpost.reshape(1, d), wpg, wpp)


ATTN_HEADS = 4
ATTN_TQ = 128
ATTN_TQ_OUTER = 1024
LOG_F32_UNDERFLOW = 104.0
ATTN_PAIR = 2
NO_WINDOW = 1e30


def _sb_attn_kernel(q_ref, gate_ref, k_ref, v_ref, o_ref, acc_ref, c_ref):
    tq, nh = ATTN_TQ, ATTN_HEADS
    lanes = nh * HEAD_DIM
    rows = nh * tq
    n_inner = q_ref.shape[0] // tq
    io = pl.program_id(2)
    lane_head = lax.broadcasted_iota(jnp.int32, (tq, lanes), 1) // HEAD_DIM
    kr = lax.broadcasted_iota(jnp.int32, (tq, 2 * tq), 0)
    kc = lax.broadcasted_iota(jnp.int32, (tq, 2 * tq), 1)
    tri_ones = jnp.where((kr > kc) | (kc >= tq), 1.0, 0.0).astype(BF16)
    tw = 2 * tq
    tri_w = jnp.where(lax.broadcasted_iota(jnp.int32, (tw, tw), 0) > lax.broadcasted_iota(jnp.int32, (tw, tw), 1),
                      1.0, 0.0).astype(BF16)
    ones_w = jnp.ones((tw, tq), BF16)
    qrow =lax.broadcasted_iota(jnp.int32, (rows, tq), 0) % tq
    kcol = lax.broadcasted_iota(jnp.int32, (rows, tq), 1)
    causal = kcol < qrow
    scale = jnp.asarray(HEAD_DIM ** -0.5, BF16)

    def tile(q4, j, c_prev, mask):
        st = pl.multiple_of(j * tq, tq)
        kj = k_ref[pl.ds(st, tq), :]
        vj = v_ref[pl.ds(st, tq), :]
        z = lax.dot_general(q4, kj, (((1,), (1,)), ((), ())), preferred_element_type=F32)
        sp = jnp.maximum(z, 0.0) + jnp.log(1.0 + jnp.exp(-jnp.abs(z)))
        if mask is not None:
            sp = jnp.where(mask, sp, 0.0)
        both = jnp.dot(sp.astype(BF16), tri_ones, preferred_element_type=F32)
        tail, rs = both[:, :tq], both[:, tq:]
        logw = (z - sp) - tail
        if c_prev is not None:
            logw = logw - c_prev
        w = jnp.exp(logw)
        if mask is not None:
            w = jnp.where(mask, w, 0.0)
        pv = jnp.dot(w.astype(BF16), vj, preferred_element_type=F32)
        return pv, rs

    def window(q4, st, c_prev):
        kw = k_ref[pl.ds(st, tw), :]
        vw = v_ref[pl.ds(st, tw), :]
        z = lax.dot_general(q4, kw, (((1,), (1,)), ((), ())), preferred_element_type=F32)
        sp = jnp.maximum(z, 0.0) + jnp.log(1.0 + jnp.exp(-jnp.abs(z)))
        spb = sp.astype(BF16)
        tail = jnp.dot(spb, tri_w, preferred_element_type=F32)
        rs = jnp.dot(spb, ones_w, preferred_element_type=F32)
        w = jnp.exp((z - sp) - tail - jnp.concatenate([c_prev, c_prev], axis=1))
        pv = jnp.dot(w.astype(BF16), vw, preferred_element_type=F32)
        return pv, rs

    def q_pair(qp, _):
        i0 = io * n_inner + ATTN_PAIR * qp
        n_windows = i0 // 2
        has_window = n_windows > 0
        q4s, mins = [], []
        for s in range(ATTN_PAIR):
            r0 = pl.multiple_of((ATTN_PAIR * qp + s) * tq, tq)
            q = q_ref[pl.ds(r0, tq), :] * scale
            q4s.append(jnp.concatenate([jnp.where(lane_head == h, q, jnp.zeros_like(q)) for h in range(nh)], axis=0))
        for s in range(ATTN_PAIR):
            pv_d, rs_d = tile(q4s[s], i0 + s, None, causal)
            st = pl.multiple_of(jnp.maximum((i0 + s) * tq - tw, 0), tq)
            pv_w, rs_w = window(q4s[s], st, jnp.where(has_window, rs_d, NO_WINDOW))
            c = rs_d + jnp.where(has_window, rs_w, 0.0)
            acc_ref[s] = pv_d + pv_w
            c_ref[s] = c
            mins.append(jnp.min(c))

        def cond(state):
            n = state[0]
            unfinished = functools.reduce(jnp.logical_or, [m < LOG_F32_UNDERFLOW for m in state[1:]])
            return (n < n_windows) & unfinished

        def body(state):
            n = state[0]
            new_mins = []
            for s in range(ATTN_PAIR):
                st = pl.multiple_of((i0 + s) * tq - (n + 1) * tw, tq)
                c_prev = c_ref[s]
                pv, rs = window(q4s[s], st, c_prev)
                acc_ref[s] += pv
                c_new = c_prev + rs
                c_ref[s] = c_new
                new_mins.append(jnp.min(c_new))
            return (n + 1, *new_mins)

        final = lax.while_loop(cond, body, (jnp.int32(1), *mins))

        @pl.when((final[0] >= n_windows) & (final[2] < LOG_F32_UNDERFLOW))
        def _():
            pv, _ = tile(q4s[1], 0, c_ref[1], None)
            acc_ref[1] += pv

        for s in range(ATTN_PAIR):
            r0 = pl.multiple_of((ATTN_PAIR * qp + s) * tq, tq)
            o = jnp.zeros((tq, lanes), F32)
            for h in range(nh):
                o = jnp.where(lane_head == h, acc_ref[s, h * tq:(h + 1) * tq, :], o)
            gate = gate_ref[pl.ds(r0, tq), :].astype(F32)
            o_ref[pl.ds(r0, tq), :] = (o * (gate * jax.nn.sigmoid(gate))).astype(o_ref.dtype)
        return 0

    lax.fori_loop(0, n_inner // ATTN_PAIR, q_pair, 0)


def _sb_attn(qg, kv):
    b, l, d2 = qg.shape
    d = d2 // 2
    lanes = ATTN_HEADS * HEAD_DIM
    nhg = d // lanes
    tqo = min(ATTN_TQ_OUTER, l)
    return pl.pallas_call(
        _sb_attn_kernel,
        grid=(b, nhg, l // tqo),
        in_specs=[
            pl.BlockSpec((None, tqo, lanes), lambda bi, hg, i: (bi, i, hg)),
            pl.BlockSpec((None, tqo, lanes), lambda bi, hg, i: (bi, i, nhg + hg)),
            pl.BlockSpec((None, l, lanes), lambda bi, hg, i: (bi, 0, hg)),
            pl.BlockSpec((None, l, lanes), lambda bi, hg, i: (bi, 0, nhg + hg)),
        ],
        out_specs=pl.BlockSpec((None, tqo, lanes), lambda bi, hg, i: (bi, i, hg)),
        out_shape=jax.ShapeDtypeStruct((b, l, d), BF16),
        scratch_shapes=[pltpu.VMEM((ATTN_PAIR, ATTN_HEADS * ATTN_TQ, lanes), F32),
                        pltpu.VMEM((ATTN_PAIR, ATTN_HEADS * ATTN_TQ, ATTN_TQ), F32)],
        compiler_params=_params("parallel", "parallel", "arbitrary"),
        name="sb_attn",
    )(qg, qg, kv, kv)


def kernel(x, p, a_norm_pre, a_norm_post, a_w_in, a_lam_re, a_lam_im, a_log_dt, a_b_re, a_b_im, a_c_re, a_c_im, a_d_skip, a_w_glu, a_b_glu, a_w_out, kv_norm, w_kv, b_norm_pre, b_norm_post, b_w_in, b_w_out, ple_w_proj, ple_w_gate):
    bsz, seq, d = x.shape
    depth = p.shape[0]
    n_a = a_w_in.shape[0]
    tokens = bsz * seq
    nc = seq // S5_CHUNK
    xf = x.reshape(tokens, d)
    p_all = p.reshape(depth, tokens, -1)
    k_v = None
    for i in range(depth):
        wpg = ple_w_gate[i].astype(BF16)
        wpp = ple_w_proj[i].astype(BF16)
        if i < n_a:
            j = i
            x4 = xf.reshape(bsz, nc, S5_CHUNK, d)
            p5 = p.reshape(depth, bsz, nc, S5_CHUNK, -1)
            wut = a_w_in[j][:, :d].T.astype(BF16)
            wg = a_w_in[j][:, d:].astype(BF16)
            ut, gate = _s5_in_proj(x4, a_norm_pre[j], wut, wg)
            s5_ops = _s5_weights(a_lam_re[j], a_lam_im[j], a_log_dt[j], a_b_re[j], a_b_im[j],
                                 a_c_re[j], a_c_im[j], a_d_skip[j])
            yt = _s5_conv(ut, *s5_ops, nc=nc)
            xf = _s5_post(x4, yt, gate, p5, i, a_w_glu[j].astype(BF16), a_b_glu[j], a_w_out[j].astype(BF16),
                          a_norm_post[j], wpg, wpp).reshape(tokens, d)
        else:
            j = i - n_a
            qg = _norm_matmul(xf, b_norm_pre[j], b_w_in[j].astype(BF16), tm=PROJ_ROWS, name="attn_in_proj")
            og = _sb_attn(qg.reshape(bsz, seq, 2 * d), k_v)
            xf = _attn_post(xf, og.reshape(tokens, d), p_all, i, b_w_out[j].astype(BF16), b_norm_post[j], wpg, wpp,
                            tm=POST_ROWS)
        if i == n_a - 1:
            k_v = _norm_matmul(xf, kv_norm, w_kv.astype(BF16), tm=PROJ_ROWS, name="kv_proj").reshape(bsz, seq, 2 * d)
    return xf.reshape(bsz, seq, d)
```

```python
import functools

import jax
import jax.numpy as jnp
from jax import lax
from jax.experimental import pallas as pl
from jax.experimental.pallas import tpu as pltpu

F32 = jnp.float32
BF16 = jnp.bfloat16

EPS = 1e-6
GROUP_SIZE = 16
STATE = 64
HEAD_DIM = 64
S5_CHUNK = 16
VMEM_LIMIT = 56 * 1024 * 1024
PROJ_ROWS = 1024
POST_ROWS = 1024


def _params(*sem):
    return pltpu.CompilerParams(dimension_semantics=sem, vmem_limit_bytes=VMEM_LIMIT)


def _rms(x, g):
    ms = jnp.mean(x * x, axis=-1, keepdims=True)
    return x * lax.rsqrt(ms + EPS) * g


def _bdot(a, b):
    return jnp.dot(a.astype(BF16), b, preferred_element_type=F32)


def _norm_matmul_kernel(x_ref, g_ref, w_ref, o_ref):
    h = _rms(x_ref[...], g_ref[...])
    o_ref[...] = _bdot(h, w_ref[...]).astype(o_ref.dtype)


def _norm_matmul(x, g, w, *, tm, name):
    m, d = x.shape
    n = w.shape[1]
    return pl.pallas_call(
        _norm_matmul_kernel,
        grid=(m // tm,),
        in_specs=[
            pl.BlockSpec((tm, d), lambda i: (i, 0)),
            pl.BlockSpec((1, d), lambda i: (0, 0)),
            pl.BlockSpec((d, n), lambda i: (0, 0)),
        ],
        out_specs=pl.BlockSpec((tm, n), lambda i: (i, 0)),
        out_shape=jax.ShapeDtypeStruct((m, n), BF16),
        compiler_params=_params("parallel"),
        name=name,
    )(x, g.reshape(1, d), w)


def _phase_rows(hbm, step, lead=()):
    return hbm.at[(*lead, step // S5_CHUNK, slice(None), step % S5_CHUNK, slice(None))]


def _fetch_phase_rows(hbm, buf, sem, k, n_steps, lead=()):
    slot = k % 2

    def copy(step, slot_):
        return pltpu.make_async_copy(_phase_rows(hbm, step, lead), buf.at[slot_], sem.at[slot_])

    @pl.when(k == 0)
    def _():
        copy(k, slot).start()

    @pl.when(k + 1 < n_steps)
    def _():
        copy(k + 1, 1 - slot).start()

    copy(k, slot).wait()
    return slot


def _s5_in_proj_kernel(x_hbm, g_ref, wut_ref, wg_ref, ut_ref, gate_ref, xbuf, sem, *, n_steps):
    slot = _fetch_phase_rows(x_hbm, xbuf, sem, pl.program_id(0), n_steps)
    hn = _rms(xbuf[slot], g_ref[...]).astype(BF16)
    ut = lax.dot_general(wut_ref[...], hn, (((1,), (1,)), ((), ())), preferred_element_type=F32)
    n_groups = ut_ref.shape[0]
    ut = ut.astype(BF16)
    for g in range(n_groups):
        ut_ref[g] = ut[g * GROUP_SIZE:(g + 1) * GROUP_SIZE, :]
    gate_ref[...] = jnp.dot(hn, wg_ref[...], preferred_element_type=F32).astype(gate_ref.dtype)


def _s5_in_proj(x4, g, wut, wg):
    bsz, nc, _, d = x4.shape
    n_groups = d // GROUP_SIZE
    n_steps = bsz * S5_CHUNK
    const = lambda k: (0, 0)
    return pl.pallas_call(
        functools.partial(_s5_in_proj_kernel, n_steps=n_steps),
        grid=(n_steps,),
        in_specs=[
            pl.BlockSpec(memory_space=pl.ANY),
            pl.BlockSpec((1, d), const),
            pl.BlockSpec((d, d), const),
            pl.BlockSpec((d, d), const),
        ],
        out_specs=[
            pl.BlockSpec((n_groups, GROUP_SIZE, nc), lambda k: (0, k % S5_CHUNK, k // S5_CHUNK)),
            pl.BlockSpec((nc, d), lambda k: (k, 0)),
        ],
        out_shape=[
            jax.ShapeDtypeStruct((n_groups, S5_CHUNK * GROUP_SIZE, bsz * nc), BF16),
            jax.ShapeDtypeStruct((n_steps * nc, d), BF16),
        ],
        scratch_shapes=[pltpu.VMEM((2, nc, d), F32), pltpu.SemaphoreType.DMA((2,))],
        compiler_params=_params("arbitrary"),
        name="s5_in_proj",
    )(x4, g.reshape(1, d), wut, wg)


S5_PAIR = 2


def _s5_conv_kernel(at_ref, wt_ref, kr_ref, vre_ref, vim_ref, ar_ref, ai_ref, yt_ref, *, nc):
    levels = max(1, (nc - 1).bit_length())
    kdim = at_ref.shape[1]
    s = [jnp.dot(wt_ref[q], at_ref[q], preferred_element_type=F32) for q in range(S5_PAIR)]
    hr = jnp.concatenate([sq[:STATE] for sq in s], axis=0).T
    hi = jnp.concatenate([sq[STATE:] for sq in s], axis=0).T
    seg = lax.broadcasted_iota(jnp.int32, hr.shape, 0) % nc
    pr, pi = ar_ref[...], ai_ref[...]
    for k in range(levels):
        d = 1 << k
        keep = seg >= d
        sr = jnp.where(keep, pltpu.roll(hr, d, axis=0), 0.0)
        si = jnp.where(keep, pltpu.roll(hi, d, axis=0), 0.0)
        hr, hi = hr + pr * sr - pi * si, hi + pr * si + pi * sr
        pr, pi = pr * pr - pi * pi, 2.0 * pr * pi
    keep = seg >= 1
    hr = jnp.where(keep, pltpu.roll(hr, 1, axis=0), 0.0).astype(BF16)
    hi = jnp.where(keep, pltpu.roll(hi, 1, axis=0), 0.0).astype(BF16)
    lane_s = lax.broadcasted_iota(jnp.int32, (kdim, kdim), 1) // GROUP_SIZE
    rep = jnp.where(lax.broadcasted_iota(jnp.int32, (2 * STATE, kdim), 0)
                    == lax.broadcasted_iota(jnp.int32, (2 * STATE, kdim), 1) % GROUP_SIZE, 1.0, 0.0).astype(BF16)
    nt = (((1,), (1,)), ((), ()))
    for q in range(S5_PAIR):
        krep = jnp.dot(kr_ref[q], rep, preferred_element_type=F32).astype(BF16)
        mt = jnp.zeros_like(krep)
        for sblk in range(S5_CHUNK):
            r = sblk * GROUP_SIZE
            shifted = krep if r == 0 else jnp.concatenate([jnp.zeros((r, kdim), krep.dtype), krep[:kdim - r]], axis=0)
            mt = jnp.where(lane_s == sblk, shifted, mt)
        y = jnp.dot(mt, at_ref[q], preferred_element_type=F32)
        y = y + lax.dot_general(vre_ref[q], hr, nt, preferred_element_type=F32)
        y = y + lax.dot_general(vim_ref[q], hi, nt, preferred_element_type=F32)
        yt_ref[q] = y.astype(yt_ref.dtype)


def _s5_conv(at, wt, krep, vre, vim, ar, ai, *, nc):
    g, k, n = at.shape
    st2 = 2 * STATE
    gs = S5_PAIR
    blk = lambda i: (i, 0, 0)
    return pl.pallas_call(
        functools.partial(_s5_conv_kernel, nc=nc),
        grid=(g // gs,),
        in_specs=[
            pl.BlockSpec((gs, k, n), blk),
            pl.BlockSpec((gs, st2, k), blk),
            pl.BlockSpec((gs, k, st2), blk),
            pl.BlockSpec((gs, k, st2), blk),
            pl.BlockSpec((gs, k, st2), blk),
            pl.BlockSpec((None, 1, st2), blk),
            pl.BlockSpec((None, 1, st2), blk),
        ],
        out_specs=pl.BlockSpec((gs, k, n), blk),
        out_shape=jax.ShapeDtypeStruct((g, k, n), BF16),
        compiler_params=_params("parallel"),
        name="s5_conv",
    )(at, wt, krep, vre, vim, ar, ai)


def _s5_weights(lam_re, lam_im, log_dt, b_re, b_im, c_re, c_im, d_skip):
    t = S5_CHUNK
    lr = jnp.minimum(lam_re.astype(F32), -1e-4)
    li = lam_im.astype(F32)
    dt = jnp.exp(log_dt.astype(F32))[:, None]
    mag = jnp.exp(lr * dt)
    a_re = mag * jnp.cos(li * dt)
    a_im = mag * jnp.sin(li * dt)
    den = lr * lr + li * li
    nr = a_re - 1.0
    f_re = (nr * lr + a_im * li) / den
    f_im = (a_im * lr - nr * li) / den
    br = b_re.astype(F32)
    bi = b_im.astype(F32)
    bb_re = f_re[..., None] * br - f_im[..., None] * bi
    bb_im = f_re[..., None] * bi + f_im[..., None] * br
    cr = c_re.astype(F32)
    ci = c_im.astype(F32)
    n_groups = cr.shape[0]

    def cpow(n):
        n = n.astype(F32)[:, None, None]
        m = jnp.exp(lr * dt * n)
        th = li * dt * n
        return m * jnp.cos(th), m * jnp.sin(th)

    pr, pi = cpow(jnp.arange(t + 1))
    prg, pig = pr.transpose(1, 0, 2)[:, :, None, :], pi.transpose(1, 0, 2)[:, :, None, :]
    car = cr[:, None] * prg - ci[:, None] * pig
    cai = cr[:, None] * pig + ci[:, None] * prg
    ca = jnp.concatenate([car[:, :t], -cai[:, :t]], axis=-1).reshape(n_groups, t * GROUP_SIZE, 2 * STATE)
    bb = jnp.concatenate([bb_re, bb_im], axis=1)
    kst = jnp.einsum('gmp,gpk->gmk', ca, bb, precision=lax.Precision.HIGHEST)
    skip = d_skip.astype(F32).reshape(n_groups, GROUP_SIZE)
    kst = kst.at[:, :GROUP_SIZE, :].add(skip[:, :, None] * jnp.eye(GROUP_SIZE, dtype=F32))
    kst = jnp.pad(kst, ((0, 0), (0, 0), (0, 2 * STATE - GROUP_SIZE)))
    qr, qi = pr[:t][::-1], pi[:t][::-1]
    w_re = jnp.einsum('sgp,gpk->gpsk', qr, bb_re) - jnp.einsum('sgp,gpk->gpsk', qi, bb_im)
    w_im = jnp.einsum('sgp,gpk->gpsk', qr, bb_im) + jnp.einsum('sgp,gpk->gpsk', qi, bb_re)
    wt = jnp.concatenate([w_re, w_im], axis=1).reshape(n_groups, 2 * STATE, t * GROUP_SIZE)
    v_re = car[:, 1:].reshape(n_groups, t * GROUP_SIZE, STATE)
    v_im = -cai[:, 1:].reshape(n_groups, t * GROUP_SIZE, STATE)
    first = (jnp.arange(n_groups) % S5_PAIR == 0)[:, None, None]

    def pack(v):
        z = jnp.zeros_like(v)
        return jnp.where(first, jnp.concatenate([v, z], axis=-1), jnp.concatenate([z, v], axis=-1)).astype(BF16)

    a_t = lambda v: v[t].reshape(n_groups // S5_PAIR, 1, S5_PAIR * STATE)
    return wt.astype(BF16), kst.astype(BF16), pack(v_re), pack(v_im), a_t(pr), a_t(pi)


def _tail(x, m, gpost, p, wpg, wpp):
    x1 = x + _rms(m, gpost)
    pg = _bdot(x1, wpg)
    pp = _bdot(p, wpp)
    return x1 + jax.nn.sigmoid(pg) * pp


def _s5_post_kernel(x_hbm, p_hbm, yt_ref, gate_ref, wglu_ref, bglu_ref, wout_ref, gpost_ref, wpg_ref, wpp_ref,
                    o_hbm, xbuf, pbuf, obuf, sem_x, sem_p, sem_o, *, n_steps, layer):
    k = pl.program_id(0)
    slot = _fetch_phase_rows(x_hbm, xbuf, sem_x, k, n_steps)
    _fetch_phase_rows(p_hbm, pbuf, sem_p, k, n_steps, lead=(layer,))
    n_groups = yt_ref.shape[0]
    yt = jnp.concatenate([yt_ref[g] for g in range(n_groups)], axis=0).astype(F32)
    g = jax.nn.gelu(yt).T
    z = _bdot(g, wglu_ref[...]) + bglu_ref[...]
    y = g * jax.nn.sigmoid(z)
    gate = gate_ref[...].astype(F32)
    y = y * (gate * jax.nn.sigmoid(gate))
    m = _bdot(y, wout_ref[...])
    res = _tail(xbuf[slot], m, gpost_ref[...], pbuf[slot], wpg_ref[...], wpp_ref[...])

    def out_copy(step, slot_):
        return pltpu.make_async_copy(obuf.at[slot_], _phase_rows(o_hbm, step), sem_o.at[slot_])

    @pl.when(k >= 2)
    def _():
        out_copy(k - 2, slot).wait()

    obuf[slot] = res
    out_copy(k, slot).start()

    @pl.when(k == n_steps - 1)
    def _():
        if n_steps >= 2:
            out_copy(k - 1, 1 - slot).wait()
        out_copy(k, slot).wait()


def _s5_post(x4, yt, gate, p5, layer, wglu, bglu, wout, gpost, wpg, wpp):
    bsz, nc, _, d = x4.shape
    pd = p5.shape[-1]
    n_groups = d // GROUP_SIZE
    n_steps = bsz * S5_CHUNK
    const = lambda k: (0, 0)
    return pl.pallas_call(
        functools.partial(_s5_post_kernel, n_steps=n_steps, layer=layer),
        grid=(n_steps,),
        in_specs=[
            pl.BlockSpec(memory_space=pl.ANY),
            pl.BlockSpec(memory_space=pl.ANY),
            pl.BlockSpec((n_groups, GROUP_SIZE, nc), lambda k: (0, k % S5_CHUNK, k // S5_CHUNK)),
            pl.BlockSpec((nc, d), lambda k: (k, 0)),
            pl.BlockSpec((d, d), const),
            pl.BlockSpec((1, d), const),
            pl.BlockSpec((d, d), const),
            pl.BlockSpec((1, d), const),
            pl.BlockSpec((d, d), const),
            pl.BlockSpec((pd, d), const),
        ],
        out_specs=pl.BlockSpec(memory_space=pl.ANY),
        out_shape=jax.ShapeDtypeStruct((bsz, nc, S5_CHUNK, d), F32),
        scratch_shapes=[pltpu.VMEM((2, nc, d), F32), pltpu.VMEM((2, nc, pd), F32), pltpu.VMEM((2, nc, d), F32),
                        pltpu.SemaphoreType.DMA((2,)), pltpu.SemaphoreType.DMA((2,)), pltpu.SemaphoreType.DMA((2,))],
        compiler_params=_params("arbitrary"),
        name="s5_post",
    )(x4, p5, yt, gate, wglu, bglu.reshape(1, d), wout, gpost.reshape(1, d), wpg, wpp)


def _attn_post_kernel(x_ref, og_ref, p_ref, wout_ref, gpost_ref, wpg_ref, wpp_ref, o_ref):
    m = jnp.dot(og_ref[...], wout_ref[...], preferred_element_type=F32)
    o_ref[...] = _tail(x_ref[...], m, gpost_ref[...], p_ref[...], wpg_ref[...], wpp_ref[...])


def _attn_post(x, og, p_all, layer, wout, gpost, wpg, wpp, *, tm):
    m, d = x.shape
    pd = p_all.shape[2]
    row = lambda i: (i, 0)
    const = lambda i: (0, 0)
    return pl.pallas_call(
        _attn_post_kernel,
        grid=(m // tm,),
        in_specs=[
            pl.BlockSpec((tm, d), row),
            pl.BlockSpec((tm, d), row),
            pl.BlockSpec((None, tm, pd), lambda i: (layer, i, 0)),
            pl.BlockSpec((d, d), const),
            pl.BlockSpec((1, d), const),
            pl.BlockSpec((d, d), const),
            pl.BlockSpec((pd, d), const),
        ],
        out_specs=pl.BlockSpec((tm, d), row),
        out_shape=jax.ShapeDtypeStruct((m, d), F32),
        compiler_params=_params("parallel"),
        name="attn_post",
    )(x, og, p_all, wout, gpost.reshape(1, d), wpg, wpp)


ATTN_HEADS = 4
ATTN_TQ = 128
ATTN_TQ_OUTER = 2048
LOG_F32_UNDERFLOW = 104.0
ATTN_PAIR = 2
NO_WINDOW = 1e30


def _sb_attn_kernel(q_ref, gate_ref, k_ref, v_ref, o_ref, acc_ref, c_ref):
    tq, nh = ATTN_TQ, ATTN_HEADS
    lanes = nh * HEAD_DIM
    rows = nh * tq
    n_inner = q_ref.shape[0] // tq
    io = pl.program_id(2)
    lane_head = lax.broadcasted_iota(jnp.int32, (tq, lanes), 1) // HEAD_DIM
    kr = lax.broadcasted_iota(jnp.int32, (tq, 2 * tq), 0)
    kc = lax.broadcasted_iota(jnp.int32, (tq, 2 * tq), 1)
    tri_ones = jnp.where((kr > kc) | (kc >= tq), 1.0, 0.0).astype(BF16)
    tw = 2 * tq
    tri_w = jnp.where(lax.broadcasted_iota(jnp.int32, (tw, tw), 0) > lax.broadcasted_iota(jnp.int32, (tw, tw), 1),
                      1.0, 0.0).astype(BF16)
    ones_w = jnp.ones((tw, tq), BF16)
    qrow =lax.broadcasted_iota(jnp.int32, (rows, tq), 0) % tq
    kcol = lax.broadcasted_iota(jnp.int32, (rows, tq), 1)
    causal = kcol < qrow
    scale = jnp.asarray(HEAD_DIM ** -0.5, BF16)

    def tile(q4, j, c_prev, mask):
        st = pl.multiple_of(j * tq, tq)
        kj = k_ref[pl.ds(st, tq), :]
        vj = v_ref[pl.ds(st, tq), :]
        z = lax.dot_general(q4, kj, (((1,), (1,)), ((), ())), preferred_element_type=F32)
        sp = jnp.maximum(z, 0.0) + jnp.log(1.0 + jnp.exp(-jnp.abs(z)))
        if mask is not None:
            sp = jnp.where(mask, sp, 0.0)
        both = jnp.dot(sp.astype(BF16), tri_ones, preferred_element_type=F32)
        tail, rs = both[:, :tq], both[:, tq:]
        logw = (z - sp) - tail
        if c_prev is not None:
            logw = logw - c_prev
        w = jnp.exp(logw)
        if mask is not None:
            w = jnp.where(mask, w, 0.0)
        pv = jnp.dot(w.astype(BF16), vj, preferred_element_type=F32)
        return pv, rs

    def window(q4, st, c_prev):
        kw = k_ref[pl.ds(st, tw), :]
        vw = v_ref[pl.ds(st, tw), :]
        z = lax.dot_general(q4, kw, (((1,), (1,)), ((), ())), preferred_element_type=F32)
        sp = jnp.maximum(z, 0.0) + jnp.log(1.0 + jnp.exp(-jnp.abs(z)))
        spb = sp.astype(BF16)
        tail = jnp.dot(spb, tri_w, preferred_element_type=F32)
        rs = jnp.dot(spb, ones_w, preferred_element_type=F32)
        w = jnp.exp((z - sp) - tail - jnp.concatenate([c_prev, c_prev], axis=1))
        pv = jnp.dot(w.astype(BF16), vw, preferred_element_type=F32)
        return pv, rs

    def q_pair(qp, _):
        i0 = io * n_inner + ATTN_PAIR * qp
        n_windows = i0 // 2
        has_window = n_windows > 0
        q4s, mins = [], []
        for s in range(ATTN_PAIR):
            r0 = pl.multiple_of((ATTN_PAIR * qp + s) * tq, tq)
            q = q_ref[pl.ds(r0, tq), :] * scale
            q4s.append(jnp.concatenate([jnp.where(lane_head == h, q, jnp.zeros_like(q)) for h in range(nh)], axis=0))
        for s in range(ATTN_PAIR):
            pv_d, rs_d = tile(q4s[s], i0 + s, None, causal)
            st = pl.multiple_of(jnp.maximum((i0 + s) * tq - tw, 0), tq)
            pv_w, rs_w = window(q4s[s], st, jnp.where(has_window, rs_d, NO_WINDOW))
            c = rs_d + jnp.where(has_window, rs_w, 0.0)
            acc_ref[s] = pv_d + pv_w
            c_ref[s] = c
            mins.append(jnp.min(c))

        def cond(state):
            n = state[0]
            unfinished = functools.reduce(jnp.logical_or, [m < LOG_F32_UNDERFLOW for m in state[1:]])
            return (n < n_windows) & unfinished

        def body(state):
            n = state[0]
            new_mins = []
            for s in range(ATTN_PAIR):
                st = pl.multiple_of((i0 + s) * tq - (n + 1) * tw, tq)
                c_prev = c_ref[s]
                pv, rs = window(q4s[s], st, c_prev)
                acc_ref[s] += pv
                c_new = c_prev + rs
                c_ref[s] = c_new
                new_mins.append(jnp.min(c_new))
            return (n + 1, *new_mins)

        final = lax.while_loop(cond, body, (jnp.int32(1), *mins))

        @pl.when((final[0] >= n_windows) & (final[2] < LOG_F32_UNDERFLOW))
        def _():
            pv, _ = tile(q4s[1], 0, c_ref[1], None)
            acc_ref[1] += pv

        for s in range(ATTN_PAIR):
            r0 = pl.multiple_of((ATTN_PAIR * qp + s) * tq, tq)
            o = jnp.zeros((tq, lanes), F32)
            for h in range(nh):
                o = jnp.where(lane_head == h, acc_ref[s, h * tq:(h + 1) * tq, :], o)
            gate = gate_ref[pl.ds(r0, tq), :].astype(F32)
            o_ref[pl.ds(r0, tq), :] = (o * (gate * jax.nn.sigmoid(gate))).astype(o_ref.dtype)
        return 0

    lax.fori_loop(0, n_inner // ATTN_PAIR, q_pair, 0)


def _sb_attn(qg, kv):
    b, l, d2 = qg.shape
    d = d2 // 2
    lanes = ATTN_HEADS * HEAD_DIM
    nhg = d // lanes
    tqo = min(ATTN_TQ_OUTER, l)
    return pl.pallas_call(
        _sb_attn_kernel,
        grid=(b, nhg, l // tqo),
        in_specs=[
            pl.BlockSpec((None, tqo, lanes), lambda bi, hg, i: (bi, i, hg)),
            pl.BlockSpec((None, tqo, lanes), lambda bi, hg, i: (bi, i, nhg + hg)),
            pl.BlockSpec((None, l, lanes), lambda bi, hg, i: (bi, 0, hg)),
            pl.BlockSpec((None, l, lanes), lambda bi, hg, i: (bi, 0, nhg + hg)),
        ],
        out_specs=pl.BlockSpec((None, tqo, lanes), lambda bi, hg, i: (bi, i, hg)),
        out_shape=jax.ShapeDtypeStruct((b, l, d), BF16),
        scratch_shapes=[pltpu.VMEM((ATTN_PAIR, ATTN_HEADS * ATTN_TQ, lanes), F32),
                        pltpu.VMEM((ATTN_PAIR, ATTN_HEADS * ATTN_TQ, ATTN_TQ), F32)],
        compiler_params=_params("parallel", "parallel", "arbitrary"),
        name="sb_attn",
    )(qg, qg, kv, kv)


def kernel(x, p, a_norm_pre, a_norm_post, a_w_in, a_lam_re, a_lam_im, a_log_dt, a_b_re, a_b_im, a_c_re, a_c_im, a_d_skip, a_w_glu, a_b_glu, a_w_out, kv_norm, w_kv, b_norm_pre, b_norm_post, b_w_in, b_w_out, ple_w_proj, ple_w_gate):
    bsz, seq, d = x.shape
    depth = p.shape[0]
    n_a = a_w_in.shape[0]
    tokens = bsz * seq
    nc = seq // S5_CHUNK
    xf = x.reshape(tokens, d)
    p_all = p.reshape(depth, tokens, -1)
    k_v = None
    for i in range(depth):
        wpg = ple_w_gate[i].astype(BF16)
        wpp = ple_w_proj[i].astype(BF16)
        if i < n_a:
            j = i
            x4 = xf.reshape(bsz, nc, S5_CHUNK, d)
            p5 = p.reshape(depth, bsz, nc, S5_CHUNK, -1)
            wut = a_w_in[j][:, :d].T.astype(BF16)
            wg = a_w_in[j][:, d:].astype(BF16)
            ut, gate = _s5_in_proj(x4, a_norm_pre[j], wut, wg)
            s5_ops = _s5_weights(a_lam_re[j], a_lam_im[j], a_log_dt[j], a_b_re[j], a_b_im[j],
                                 a_c_re[j], a_c_im[j], a_d_skip[j])
            yt = _s5_conv(ut, *s5_ops, nc=nc)
            xf = _s5_post(x4, yt, gate, p5, i, a_w_glu[j].astype(BF16), a_b_glu[j], a_w_out[j].astype(BF16),
                          a_norm_post[j], wpg, wpp).reshape(tokens, d)
        else:
            j = i - n_a
            qg = _norm_matmul(xf, b_norm_pre[j], b_w_in[j].astype(BF16), tm=PROJ_ROWS, name="attn_in_proj")
            og = _sb_attn(qg.reshape(bsz, seq, 2 * d), k_v)
            xf = _attn_post(xf, og.reshape(tokens, d), p_all, i, b_w_out[j].astype(BF16), b_norm_post[j], wpg, wpp,
                            tm=POST_ROWS)
        if i == n_a - 1:
            k_v = _norm_matmul(xf, kv_norm, w_kv.astype(BF16), tm=PROJ_ROWS, name="kv_proj").reshape(bsz, seq, 2 * d)
    return xf.reshape(bsz, seq, d)
```

```python
import functools

import jax
import jax.numpy as jnp
from jax import lax
from jax.experimental import pallas as pl
from jax.experimental.pallas import tpu as pltpu

F32 = jnp.float32
BF16 = jnp.bfloat16

EPS = 1e-6
GROUP_SIZE = 16
STATE = 64
HEAD_DIM = 64
S5_CHUNK = 16
VMEM_LIMIT = 56 * 1024 * 1024
PROJ_ROWS = 1024
POST_ROWS = 1024


def _params(*sem):
    return pltpu.CompilerParams(dimension_semantics=sem, vmem_limit_bytes=VMEM_LIMIT)


def _rms(x, g):
    ms = jnp.mean(x * x, axis=-1, keepdims=True)
    return x * lax.rsqrt(ms + EPS) * g


def _bdot(a, b):
    return jnp.dot(a.astype(BF16), b, preferred_element_type=F32)


def _norm_matmul_kernel(x_ref, g_ref, w_ref, o_ref):
    h = _rms(x_ref[...], g_ref[...])
    o_ref[...] = _bdot(h, w_ref[...]).astype(o_ref.dtype)


def _norm_matmul(x, g, w, *, tm, name):
    m, d = x.shape
    n = w.shape[1]
    return pl.pallas_call(
        _norm_matmul_kernel,
        grid=(m // tm,),
        in_specs=[
            pl.BlockSpec((tm, d), lambda i: (i, 0)),
            pl.BlockSpec((1, d), lambda i: (0, 0)),
            pl.BlockSpec((d, n), lambda i: (0, 0)),
        ],
        out_specs=pl.BlockSpec((tm, n), lambda i: (i, 0)),
        out_shape=jax.ShapeDtypeStruct((m, n), BF16),
        compiler_params=_params("parallel"),
        name=name,
    )(x, g.reshape(1, d), w)


def _phase_rows(hbm, step, lead=()):
    return hbm.at[(*lead, step // S5_CHUNK, slice(None), step % S5_CHUNK, slice(None))]


def _fetch_phase_rows(hbm, buf, sem, k, n_steps, lead=()):
    slot = k % 2

    def copy(step, slot_):
        return pltpu.make_async_copy(_phase_rows(hbm, step, lead), buf.at[slot_], sem.at[slot_])

    @pl.when(k == 0)
    def _():
        copy(k, slot).start()

    @pl.when(k + 1 < n_steps)
    def _():
        copy(k + 1, 1 - slot).start()

    copy(k, slot).wait()
    return slot


def _s5_in_proj_kernel(x_hbm, g_ref, wut_ref, wg_ref, ut_ref, gate_ref, xbuf, sem, *, n_steps):
    slot = _fetch_phase_rows(x_hbm, xbuf, sem, pl.program_id(0), n_steps)
    hn = _rms(xbuf[slot], g_ref[...]).astype(BF16)
    ut = lax.dot_general(wut_ref[...], hn, (((1,), (1,)), ((), ())), preferred_element_type=F32)
    n_groups = ut_ref.shape[0]
    ut = ut.astype(BF16)
    for g in range(n_groups):
        ut_ref[g] = ut[g * GROUP_SIZE:(g + 1) * GROUP_SIZE, :]
    gate_ref[...] = jnp.dot(hn, wg_ref[...], preferred_element_type=F32).astype(gate_ref.dtype)


def _s5_in_proj(x4, g, wut, wg):
    bsz, nc, _, d = x4.shape
    n_groups = d // GROUP_SIZE
    n_steps = bsz * S5_CHUNK
    const = lambda k: (0, 0)
    return pl.pallas_call(
        functools.partial(_s5_in_proj_kernel, n_steps=n_steps),
        grid=(n_steps,),
        in_specs=[
            pl.BlockSpec(memory_space=pl.ANY),
            pl.BlockSpec((1, d), const),
            pl.BlockSpec((d, d), const),
            pl.BlockSpec((d, d), const),
        ],
        out_specs=[
            pl.BlockSpec((n_groups, GROUP_SIZE, nc), lambda k: (0, k % S5_CHUNK, k // S5_CHUNK)),
            pl.BlockSpec((nc, d), lambda k: (k, 0)),
        ],
        out_shape=[
            jax.ShapeDtypeStruct((n_groups, S5_CHUNK * GROUP_SIZE, bsz * nc), BF16),
            jax.ShapeDtypeStruct((n_steps * nc, d), BF16),
        ],
        scratch_shapes=[pltpu.VMEM((2, nc, d), F32), pltpu.SemaphoreType.DMA((2,))],
        compiler_params=_params("arbitrary"),
        name="s5_in_proj",
    )(x4, g.reshape(1, d), wut, wg)


S5_PAIR = 2


def _s5_conv_kernel(at_ref, wt_ref, kr_ref, vre_ref, vim_ref, ar_ref, ai_ref, yt_ref, *, nc):
    levels = max(1, (nc - 1).bit_length())
    kdim = at_ref.shape[1]
    s = [jnp.dot(wt_ref[q], at_ref[q], preferred_element_type=F32) for q in range(S5_PAIR)]
    hr = jnp.concatenate([sq[:STATE] for sq in s], axis=0).T
    hi = jnp.concatenate([sq[STATE:] for sq in s], axis=0).T
    seg = lax.broadcasted_iota(jnp.int32, hr.shape, 0) % nc
    pr, pi = ar_ref[...], ai_ref[...]
    for k in range(levels):
        d = 1 << k
        keep = seg >= d
        sr = jnp.where(keep, pltpu.roll(hr, d, axis=0), 0.0)
        si = jnp.where(keep, pltpu.roll(hi, d, axis=0), 0.0)
        hr, hi = hr + pr * sr - pi * si, hi + pr * si + pi * sr
        pr, pi = pr * pr - pi * pi, 2.0 * pr * pi
    keep = seg >= 1
    hr = jnp.where(keep, pltpu.roll(hr, 1, axis=0), 0.0).astype(BF16)
    hi = jnp.where(keep, pltpu.roll(hi, 1, axis=0), 0.0).astype(BF16)
    lane_s = lax.broadcasted_iota(jnp.int32, (kdim, kdim), 1) // GROUP_SIZE
    rep = jnp.where(lax.broadcasted_iota(jnp.int32, (2 * STATE, kdim), 0)
                    == lax.broadcasted_iota(jnp.int32, (2 * STATE, kdim), 1) % GROUP_SIZE, 1.0, 0.0).astype(BF16)
    nt = (((1,), (1,)), ((), ()))
    for q in range(S5_PAIR):
        krep = jnp.dot(kr_ref[q], rep, preferred_element_type=F32).astype(BF16)
        mt = jnp.zeros_like(krep)
        for sblk in range(S5_CHUNK):
            r = sblk * GROUP_SIZE
            shifted = krep if r == 0 else jnp.concatenate([jnp.zeros((r, kdim), krep.dtype), krep[:kdim - r]], axis=0)
            mt = jnp.where(lane_s == sblk, shifted, mt)
        y = jnp.dot(mt, at_ref[q], preferred_element_type=F32)
        y = y + lax.dot_general(vre_ref[q], hr, nt, preferred_element_type=F32)
        y = y + lax.dot_general(vim_ref[q], hi, nt, preferred_element_type=F32)
        yt_ref[q] = y.astype(yt_ref.dtype)


def _s5_conv(at, wt, krep, vre, vim, ar, ai, *, nc):
    g, k, n = at.shape
    st2 = 2 * STATE
    gs = S5_PAIR
    blk = lambda i: (i, 0, 0)
    return pl.pallas_call(
        functools.partial(_s5_conv_kernel, nc=nc),
        grid=(g // gs,),
        in_specs=[
            pl.BlockSpec((gs, k, n), blk),
            pl.BlockSpec((gs, st2, k), blk),
            pl.BlockSpec((gs, k, st2), blk),
            pl.BlockSpec((gs, k, st2), blk),
            pl.BlockSpec((gs, k, st2), blk),
            pl.BlockSpec((None, 1, st2), blk),
            pl.BlockSpec((None, 1, st2), blk),
        ],
        out_specs=pl.BlockSpec((gs, k, n), blk),
        out_shape=jax.ShapeDtypeStruct((g, k, n), BF16),
        compiler_params=_params("parallel"),
        name="s5_conv",
    )(at, wt, krep, vre, vim, ar, ai)


def _s5_weights(lam_re, lam_im, log_dt, b_re, b_im, c_re, c_im, d_skip):
    t = S5_CHUNK
    lr = jnp.minimum(lam_re.astype(F32), -1e-4)
    li = lam_im.astype(F32)
    dt = jnp.exp(log_dt.astype(F32))[:, None]
    mag = jnp.exp(lr * dt)
    a_re = mag * jnp.cos(li * dt)
    a_im = mag * jnp.sin(li * dt)
    den = lr * lr + li * li
    nr = a_re - 1.0
    f_re = (nr * lr + a_im * li) / den
    f_im = (a_im * lr - nr * li) / den
    br = b_re.astype(F32)
    bi = b_im.astype(F32)
    bb_re = f_re[..., None] * br - f_im[..., None] * bi
    bb_im = f_re[..., None] * bi + f_im[..., None] * br
    cr = c_re.astype(F32)
    ci = c_im.astype(F32)
    n_groups = cr.shape[0]

    def cpow(n):
        n = n.astype(F32)[:, None, None]
        m = jnp.exp(lr * dt * n)
        th = li * dt * n
        return m * jnp.cos(th), m * jnp.sin(th)

    pr, pi = cpow(jnp.arange(t + 1))
    prg, pig = pr.transpose(1, 0, 2)[:, :, None, :], pi.transpose(1, 0, 2)[:, :, None, :]
    cc = (jnp.concatenate([cr, cr], axis=-1)[:, None] * jnp.concatenate([prg, -pig], axis=-1)
          - jnp.concatenate([ci, ci], axis=-1)[:, None] * jnp.concatenate([pig, prg], axis=-1))
    ca = cc[:, :t].reshape(n_groups, t * GROUP_SIZE, 2 * STATE)
    bb = jnp.concatenate([bb_re, bb_im], axis=1)
    kst = jnp.einsum('gmp,gpk->gmk', ca, bb, precision=lax.Precision.HIGHEST)
    skip = d_skip.astype(F32).reshape(n_groups, GROUP_SIZE)
    kst = kst.at[:, :GROUP_SIZE, :].add(skip[:, :, None] * jnp.eye(GROUP_SIZE, dtype=F32))
    kst = jnp.pad(kst, ((0, 0), (0, 0), (0, 2 * STATE - GROUP_SIZE)))
    qr, qi = pr[:t][::-1], pi[:t][::-1]
    w_re = jnp.einsum('sgp,gpk->gpsk', qr, bb_re) - jnp.einsum('sgp,gpk->gpsk', qi, bb_im)
    w_im = jnp.einsum('sgp,gpk->gpsk', qr, bb_im) + jnp.einsum('sgp,gpk->gpsk', qi, bb_re)
    wt = jnp.concatenate([w_re, w_im], axis=1).reshape(n_groups, 2 * STATE, t * GROUP_SIZE)
    v = cc[:, 1:].reshape(n_groups, t * GROUP_SIZE, 2 * STATE)
    vs = jnp.roll(v, STATE, axis=-1)
    first = (jnp.arange(n_groups) % S5_PAIR == 0)[:, None, None]
    lo = (jnp.arange(2 * STATE) < STATE)[None, None, :]
    vre = jnp.where(first, jnp.where(lo, v, 0.0), jnp.where(lo, 0.0, vs)).astype(BF16)
    vim = jnp.where(first, jnp.where(lo, vs, 0.0), jnp.where(lo, 0.0, v)).astype(BF16)
    a_t = lambda x: x[t].reshape(n_groups // S5_PAIR, 1, S5_PAIR * STATE)
    return wt.astype(BF16), kst.astype(BF16), vre, vim, a_t(pr), a_t(pi)


def _tail(x, m, gpost, p, wpg, wpp):
    x1 = x + _rms(m, gpost)
    pg = _bdot(x1, wpg)
    pp = _bdot(p, wpp)
    return x1 + jax.nn.sigmoid(pg) * pp


def _s5_post_kernel(x_hbm, p_hbm, yt_ref, gate_ref, wglu_ref, bglu_ref, wout_ref, gpost_ref, wpg_ref, wpp_ref,
                    o_hbm, xbuf, pbuf, obuf, sem_x, sem_p, sem_o, *, n_steps, layer):
    k = pl.program_id(0)
    slot = _fetch_phase_rows(x_hbm, xbuf, sem_x, k, n_steps)
    _fetch_phase_rows(p_hbm, pbuf, sem_p, k, n_steps, lead=(layer,))
    n_groups = yt_ref.shape[0]
    yt = jnp.concatenate([yt_ref[g] for g in range(n_groups)], axis=0).astype(F32)
    g = jax.nn.gelu(yt).T
    z = _bdot(g, wglu_ref[...]) + bglu_ref[...]
    y = g * jax.nn.sigmoid(z)
    gate = gate_ref[...].astype(F32)
    y = y * (gate * jax.nn.sigmoid(gate))
    m = _bdot(y, wout_ref[...])
    res = _tail(xbuf[slot], m, gpost_ref[...], pbuf[slot], wpg_ref[...], wpp_ref[...])

    def out_copy(step, slot_):
        return pltpu.make_async_copy(obuf.at[slot_], _phase_rows(o_hbm, step), sem_o.at[slot_])

    @pl.when(k >= 2)
    def _():
        out_copy(k - 2, slot).wait()

    obuf[slot] = res
    out_copy(k, slot).start()

    @pl.when(k == n_steps - 1)
    def _():
        if n_steps >= 2:
            out_copy(k - 1, 1 - slot).wait()
        out_copy(k, slot).wait()


def _s5_post(x4, yt, gate, p5, layer, wglu, bglu, wout, gpost, wpg, wpp):
    bsz, nc, _, d = x4.shape
    pd = p5.shape[-1]
    n_groups = d // GROUP_SIZE
    n_steps = bsz * S5_CHUNK
    const = lambda k: (0, 0)
    return pl.pallas_call(
        functools.partial(_s5_post_kernel, n_steps=n_steps, layer=layer),
        grid=(n_steps,),
        in_specs=[
            pl.BlockSpec(memory_space=pl.ANY),
            pl.BlockSpec(memory_space=pl.ANY),
            pl.BlockSpec((n_groups, GROUP_SIZE, nc), lambda k: (0, k % S5_CHUNK, k // S5_CHUNK)),
            pl.BlockSpec((nc, d), lambda k: (k, 0)),
            pl.BlockSpec((d, d), const),
            pl.BlockSpec((1, d), const),
            pl.BlockSpec((d, d), const),
            pl.BlockSpec((1, d), const),
            pl.BlockSpec((d, d), const),
            pl.BlockSpec((pd, d), const),
        ],
        out_specs=pl.BlockSpec(memory_space=pl.ANY),
        out_shape=jax.ShapeDtypeStruct((bsz, nc, S5_CHUNK, d), F32),
        scratch_shapes=[pltpu.VMEM((2, nc, d), F32), pltpu.VMEM((2, nc, pd), F32), pltpu.VMEM((2, nc, d), F32),
                        pltpu.SemaphoreType.DMA((2,)), pltpu.SemaphoreType.DMA((2,)), pltpu.SemaphoreType.DMA((2,))],
        compiler_params=_params("arbitrary"),
        name="s5_post",
    )(x4, p5, yt, gate, wglu, bglu.reshape(1, d), wout, gpost.reshape(1, d), wpg, wpp)


def _attn_post_kernel(x_ref, og_ref, p_ref, wout_ref, gpost_ref, wpg_ref, wpp_ref, o_ref):
    m = jnp.dot(og_ref[...], wout_ref[...], preferred_element_type=F32)
    o_ref[...] = _tail(x_ref[...], m, gpost_ref[...], p_ref[...], wpg_ref[...], wpp_ref[...])


def _attn_post(x, og, p_all, layer, wout, gpost, wpg, wpp, *, tm):
    m, d = x.shape
    pd = p_all.shape[2]
    row = lambda i: (i, 0)
    const = lambda i: (0, 0)
    return pl.pallas_call(
        _attn_post_kernel,
        grid=(m // tm,),
        in_specs=[
            pl.BlockSpec((tm, d), row),
            pl.BlockSpec((tm, d), row),
            pl.BlockSpec((None, tm, pd), lambda i: (layer, i, 0)),
            pl.BlockSpec((d, d), const),
            pl.BlockSpec((1, d), const),
            pl.BlockSpec((d, d), const),
            pl.BlockSpec((pd, d), const),
        ],
        out_specs=pl.BlockSpec((tm, d), row),
        out_shape=jax.ShapeDtypeStruct((m, d), F32),
        compiler_params=_params("parallel"),
        name="attn_post",
    )(x, og, p_all, wout, gpost.reshape(1, d), wpg, wpp)


ATTN_HEADS = 4
ATTN_TQ = 128
ATTN_TQ_OUTER = 2048
LOG_F32_UNDERFLOW = 104.0
ATTN_PAIR = 2
NO_WINDOW = 1e30


def _sb_attn_kernel(q_ref, gate_ref, k_ref, v_ref, o_ref, acc_ref, c_ref):
    tq, nh = ATTN_TQ, ATTN_HEADS
    lanes = nh * HEAD_DIM
    rows = nh * tq
    n_inner = q_ref.shape[0] // tq
    io = pl.program_id(2)
    lane_head = lax.broadcasted_iota(jnp.int32, (tq, lanes), 1) // HEAD_DIM
    kr = lax.broadcasted_iota(jnp.int32, (tq, 2 * tq), 0)
    kc = lax.broadcasted_iota(jnp.int32, (tq, 2 * tq), 1)
    tri_ones = jnp.where((kr > kc) | (kc >= tq), 1.0, 0.0).astype(BF16)
    tw = 2 * tq
    tri_w = jnp.where(lax.broadcasted_iota(jnp.int32, (tw, tw), 0) > lax.broadcasted_iota(jnp.int32, (tw, tw), 1),
                      1.0, 0.0).astype(BF16)
    ones_w = jnp.ones((tw, tq), BF16)
    qrow =lax.broadcasted_iota(jnp.int32, (rows, tq), 0) % tq
    kcol = lax.broadcasted_iota(jnp.int32, (rows, tq), 1)
    causal = kcol < qrow
    scale = jnp.asarray(HEAD_DIM ** -0.5, BF16)

    def tile(q4, j, c_prev, mask):
        st = pl.multiple_of(j * tq, tq)
        kj = k_ref[pl.ds(st, tq), :]
        vj = v_ref[pl.ds(st, tq), :]
        z = lax.dot_general(q4, kj, (((1,), (1,)), ((), ())), preferred_element_type=F32)
        sp = jnp.maximum(z, 0.0) + jnp.log(1.0 + jnp.exp(-jnp.abs(z)))
        if mask is not None:
            sp = jnp.where(mask, sp, 0.0)
        both = jnp.dot(sp.astype(BF16), tri_ones, preferred_element_type=F32)
        tail, rs = both[:, :tq], both[:, tq:]
        logw = (z - sp) - tail
        if c_prev is not None:
            logw = logw - c_prev
        w = jnp.exp(logw)
        if mask is not None:
            w = jnp.where(mask, w, 0.0)
        pv = jnp.dot(w.astype(BF16), vj, preferred_element_type=F32)
        return pv, rs

    def window(q4, st, c_prev):
        kw = k_ref[pl.ds(st, tw), :]
        vw = v_ref[pl.ds(st, tw), :]
        z = lax.dot_general(q4, kw, (((1,), (1,)), ((), ())), preferred_element_type=F32)
        sp = jnp.maximum(z, 0.0) + jnp.log(1.0 + jnp.exp(-jnp.abs(z)))
        spb = sp.astype(BF16)
        tail = jnp.dot(spb, tri_w, preferred_element_type=F32)
        rs = jnp.dot(spb, ones_w, preferred_element_type=F32)
        w = jnp.exp((z - sp) - tail - jnp.concatenate([c_prev, c_prev], axis=1))
        pv = jnp.dot(w.astype(BF16), vw, preferred_element_type=F32)
        return pv, rs

    def q_pair(qp, _):
        i0 = io * n_inner + ATTN_PAIR * qp
        n_windows = i0 // 2
        has_window = n_windows > 0
        q4s, mins = [], []
        for s in range(ATTN_PAIR):
            r0 = pl.multiple_of((ATTN_PAIR * qp + s) * tq, tq)
            q = q_ref[pl.ds(r0, tq), :] * scale
            q4s.append(jnp.concatenate([jnp.where(lane_head == h, q, jnp.zeros_like(q)) for h in range(nh)], axis=0))
        for s in range(ATTN_PAIR):
            pv_d, rs_d = tile(q4s[s], i0 + s, None, causal)
            st = pl.multiple_of(jnp.maximum((i0 + s) * tq - tw, 0), tq)
            pv_w, rs_w = window(q4s[s], st, jnp.where(has_window, rs_d, NO_WINDOW))
            c = rs_d + jnp.where(has_window, rs_w, 0.0)
            acc_ref[s] = pv_d + pv_w
            c_ref[s] = c
            mins.append(jnp.min(c))

        def cond(state):
            n = state[0]
            unfinished = functools.reduce(jnp.logical_or, [m < LOG_F32_UNDERFLOW for m in state[1:]])
            return (n < n_windows) & unfinished

        def body(state):
            n = state[0]
            new_mins = []
            for s in range(ATTN_PAIR):
                st = pl.multiple_of((i0 + s) * tq - (n + 1) * tw, tq)
                c_prev = c_ref[s]
                pv, rs = window(q4s[s], st, c_prev)
                acc_ref[s] += pv
                c_new = c_prev + rs
                c_ref[s] = c_new
                new_mins.append(jnp.min(c_new))
            return (n + 1, *new_mins)

        final = lax.while_loop(cond, body, (jnp.int32(1), *mins))

        @pl.when((final[0] >= n_windows) & (final[2] < LOG_F32_UNDERFLOW))
        def _():
            pv, _ = tile(q4s[1], 0, c_ref[1], None)
            acc_ref[1] += pv

        for s in range(ATTN_PAIR):
            r0 = pl.multiple_of((ATTN_PAIR * qp + s) * tq, tq)
            o = jnp.zeros((tq, lanes), F32)
            for h in range(nh):
                o = jnp.where(lane_head == h, acc_ref[s, h * tq:(h + 1) * tq, :], o)
            gate = gate_ref[pl.ds(r0, tq), :].astype(F32)
            o_ref[pl.ds(r0, tq), :] = (o * (gate * jax.nn.sigmoid(gate))).astype(o_ref.dtype)
        return 0

    lax.fori_loop(0, n_inner // ATTN_PAIR, q_pair, 0)


def _sb_attn(qg, kv):
    b, l, d2 = qg.shape
    d = d2 // 2
    lanes = ATTN_HEADS * HEAD_DIM
    nhg = d // lanes
    tqo = min(ATTN_TQ_OUTER, l)
    return pl.pallas_call(
        _sb_attn_kernel,
        grid=(b, nhg, l // tqo),
        in_specs=[
            pl.BlockSpec((None, tqo, lanes), lambda bi, hg, i: (bi, i, hg)),
            pl.BlockSpec((None, tqo, lanes), lambda bi, hg, i: (bi, i, nhg + hg)),
            pl.BlockSpec((None, l, lanes), lambda bi, hg, i: (bi, 0, hg)),
            pl.BlockSpec((None, l, lanes), lambda bi, hg, i: (bi, 0, nhg + hg)),
        ],
        out_specs=pl.BlockSpec((None, tqo, lanes), lambda bi, hg, i: (bi, i, hg)),
        out_shape=jax.ShapeDtypeStruct((b, l, d), BF16),
        scratch_shapes=[pltpu.VMEM((ATTN_PAIR, ATTN_HEADS * ATTN_TQ, lanes), F32),
                        pltpu.VMEM((ATTN_PAIR, ATTN_HEADS * ATTN_TQ, ATTN_TQ), F32)],
        compiler_params=_params("parallel", "parallel", "arbitrary"),
        name="sb_attn",
    )(qg, qg, kv, kv)


def kernel(x, p, a_norm_pre, a_norm_post, a_w_in, a_lam_re, a_lam_im, a_log_dt, a_b_re, a_b_im, a_c_re, a_c_im, a_d_skip, a_w_glu, a_b_glu, a_w_out, kv_norm, w_kv, b_norm_pre, b_norm_post, b_w_in, b_w_out, ple_w_proj, ple_w_gate):
    bsz, seq, d = x.shape
    depth = p.shape[0]
    n_a = a_w_in.shape[0]
    tokens = bsz * seq
    nc = seq // S5_CHUNK
    xf = x.reshape(tokens, d)
    p_all = p.reshape(depth, tokens, -1)
    k_v = None
    for i in range(depth):
        wpg = ple_w_gate[i].astype(BF16)
        wpp = ple_w_proj[i].astype(BF16)
        if i < n_a:
            j = i
            x4 = xf.reshape(bsz, nc, S5_CHUNK, d)
            p5 = p.reshape(depth, bsz, nc, S5_CHUNK, -1)
            wut = a_w_in[j][:, :d].T.astype(BF16)
            wg = a_w_in[j][:, d:].astype(BF16)
            ut, gate = _s5_in_proj(x4, a_norm_pre[j], wut, wg)
            s5_ops = _s5_weights(a_lam_re[j], a_lam_im[j], a_log_dt[j], a_b_re[j], a_b_im[j],
                                 a_c_re[j], a_c_im[j], a_d_skip[j])
            yt = _s5_conv(ut, *s5_ops, nc=nc)
            xf = _s5_post(x4, yt, gate, p5, i, a_w_glu[j].astype(BF16), a_b_glu[j], a_w_out[j].astype(BF16),
                          a_norm_post[j], wpg, wpp).reshape(tokens, d)
        else:
            j = i - n_a
            qg = _norm_matmul(xf, b_norm_pre[j], b_w_in[j].astype(BF16), tm=PROJ_ROWS, name="attn_in_proj")
            og = _sb_attn(qg.reshape(bsz, seq, 2 * d), k_v)
            xf = _attn_post(xf, og.reshape(tokens, d), p_all, i, b_w_out[j].astype(BF16), b_norm_post[j], wpg, wpp,
                            tm=POST_ROWS)
        if i == n_a - 1:
            k_v = _norm_matmul(xf, kv_norm, w_kv.astype(BF16), tm=PROJ_ROWS, name="kv_proj").reshape(bsz, seq, 2 * d)
    return xf.reshape(bsz, seq, d)
```

```python
import functools

import jax
import jax.numpy as jnp
from jax import lax
from jax.experimental import pallas as pl
from jax.experimental.pallas import tpu as pltpu

F32 = jnp.float32
BF16 = jnp.bfloat16

EPS = 1e-6
GROUP_SIZE = 16
STATE = 64
HEAD_DIM = 64
S5_CHUNK = 16
VMEM_LIMIT = 56 * 1024 * 1024
PROJ_ROWS = 1024
POST_ROWS = 1024


def _params(*sem):
    return pltpu.CompilerParams(dimension_semantics=sem, vmem_limit_bytes=VMEM_LIMIT)


def _rms(x, g):
    ms = jnp.mean(x * x, axis=-1, keepdims=True)
    return x * lax.rsqrt(ms + EPS) * g


def _bdot(a, b):
    return jnp.dot(a.astype(BF16), b, preferred_element_type=F32)


def _norm_matmul_kernel(x_ref, g_ref, w_ref, o_ref):
    h = _rms(x_ref[...], g_ref[...])
    o_ref[...] = _bdot(h, w_ref[...]).astype(o_ref.dtype)


def _norm_matmul(x, g, w, *, tm, name):
    m, d = x.shape
    n = w.shape[1]
    return pl.pallas_call(
        _norm_matmul_kernel,
        grid=(m // tm,),
        in_specs=[
            pl.BlockSpec((tm, d), lambda i: (i, 0)),
            pl.BlockSpec((1, d), lambda i: (0, 0)),
            pl.BlockSpec((d, n), lambda i: (0, 0)),
        ],
        out_specs=pl.BlockSpec((tm, n), lambda i: (i, 0)),
        out_shape=jax.ShapeDtypeStruct((m, n), BF16),
        compiler_params=_params("parallel"),
        name=name,
    )(x, g.reshape(1, d), w)


def _phase_rows(hbm, step, lead=()):
    return hbm.at[(*lead, step // S5_CHUNK, slice(None), step % S5_CHUNK, slice(None))]


def _fetch_phase_rows(hbm, buf, sem, k, n_steps, lead=()):
    slot = k % 2

    def copy(step, slot_):
        return pltpu.make_async_copy(_phase_rows(hbm, step, lead), buf.at[slot_], sem.at[slot_])

    @pl.when(k == 0)
    def _():
        copy(k, slot).start()

    @pl.when(k + 1 < n_steps)
    def _():
        copy(k + 1, 1 - slot).start()

    copy(k, slot).wait()
    return slot


def _s5_in_proj_kernel(x_hbm, g_ref, wut_ref, wg_ref, ut_ref, gate_ref, xbuf, sem, *, n_steps):
    slot = _fetch_phase_rows(x_hbm, xbuf, sem, pl.program_id(0), n_steps)
    hn = _rms(xbuf[slot], g_ref[...]).astype(BF16)
    ut = lax.dot_general(wut_ref[...], hn, (((1,), (1,)), ((), ())), preferred_element_type=F32)
    n_groups = ut_ref.shape[0]
    ut = ut.astype(BF16)
    for g in range(n_groups):
        ut_ref[g] = ut[g * GROUP_SIZE:(g + 1) * GROUP_SIZE, :]
    gate_ref[...] = jnp.dot(hn, wg_ref[...], preferred_element_type=F32).astype(gate_ref.dtype)


def _s5_in_proj(x4, g, wut, wg):
    bsz, nc, _, d = x4.shape
    n_groups = d // GROUP_SIZE
    n_steps = bsz * S5_CHUNK
    const = lambda k: (0, 0)
    return pl.pallas_call(
        functools.partial(_s5_in_proj_kernel, n_steps=n_steps),
        grid=(n_steps,),
        in_specs=[
            pl.BlockSpec(memory_space=pl.ANY),
            pl.BlockSpec((1, d), const),
            pl.BlockSpec((d, d), const),
            pl.BlockSpec((d, d), const),
        ],
        out_specs=[
            pl.BlockSpec((n_groups, GROUP_SIZE, nc), lambda k: (0, k % S5_CHUNK, k // S5_CHUNK)),
            pl.BlockSpec((nc, d), lambda k: (k, 0)),
        ],
        out_shape=[
            jax.ShapeDtypeStruct((n_groups, S5_CHUNK * GROUP_SIZE, bsz * nc), BF16),
            jax.ShapeDtypeStruct((n_steps * nc, d), BF16),
        ],
        scratch_shapes=[pltpu.VMEM((2, nc, d), F32), pltpu.SemaphoreType.DMA((2,))],
        compiler_params=_params("arbitrary"),
        name="s5_in_proj",
    )(x4, g.reshape(1, d), wut, wg)


S5_PAIR = 2


def _s5_conv_kernel(at_ref, wt_ref, kr_ref, vre_ref, vim_ref, ar_ref, ai_ref, yt_ref, *, nc):
    levels = max(1, (nc - 1).bit_length())
    kdim = at_ref.shape[1]
    s = [jnp.dot(wt_ref[q], at_ref[q], preferred_element_type=F32) for q in range(S5_PAIR)]
    hr = jnp.concatenate([sq[:STATE] for sq in s], axis=0).T
    hi = jnp.concatenate([sq[STATE:] for sq in s], axis=0).T
    seg = lax.broadcasted_iota(jnp.int32, hr.shape, 0) % nc
    pr, pi = ar_ref[...], ai_ref[...]
    for k in range(levels):
        d = 1 << k
        keep = seg >= d
        sr = jnp.where(keep, pltpu.roll(hr, d, axis=0), 0.0)
        si = jnp.where(keep, pltpu.roll(hi, d, axis=0), 0.0)
        hr, hi = hr + pr * sr - pi * si, hi + pr * si + pi * sr
        pr, pi = pr * pr - pi * pi, 2.0 * pr * pi
    keep = seg >= 1
    hr = jnp.where(keep, pltpu.roll(hr, 1, axis=0), 0.0).astype(BF16)
    hi = jnp.where(keep, pltpu.roll(hi, 1, axis=0), 0.0).astype(BF16)
    lane_s = lax.broadcasted_iota(jnp.int32, (kdim, kdim), 1) // GROUP_SIZE
    rep = jnp.where(lax.broadcasted_iota(jnp.int32, (2 * STATE, kdim), 0)
                    == lax.broadcasted_iota(jnp.int32, (2 * STATE, kdim), 1) % GROUP_SIZE, 1.0, 0.0).astype(BF16)
    nt = (((1,), (1,)), ((), ()))
    for q in range(S5_PAIR):
        krep = jnp.dot(kr_ref[q], rep, preferred_element_type=F32).astype(BF16)
        mt = jnp.zeros_like(krep)
        for sblk in range(S5_CHUNK):
            r = sblk * GROUP_SIZE
            shifted = krep if r == 0 else jnp.concatenate([jnp.zeros((r, kdim), krep.dtype), krep[:kdim - r]], axis=0)
            mt = jnp.where(lane_s == sblk, shifted, mt)
        y = jnp.dot(mt, at_ref[q], preferred_element_type=F32)
        y = y + lax.dot_general(vre_ref[q], hr, nt, preferred_element_type=F32)
        y = y + lax.dot_general(vim_ref[q], hi, nt, preferred_element_type=F32)
        yt_ref[q] = y.astype(yt_ref.dtype)


def _s5_conv(at, wt, krep, vre, vim, ar, ai, *, nc):
    g, k, n = at.shape
    st2 = 2 * STATE
    gs = S5_PAIR
    blk = lambda i: (i, 0, 0)
    return pl.pallas_call(
        functools.partial(_s5_conv_kernel, nc=nc),
        grid=(g // gs,),
        in_specs=[
            pl.BlockSpec((gs, k, n), blk),
            pl.BlockSpec((gs, st2, k), blk),
            pl.BlockSpec((gs, k, st2), blk),
            pl.BlockSpec((gs, k, st2), blk),
            pl.BlockSpec((gs, k, st2), blk),
            pl.BlockSpec((None, 1, st2), blk),
            pl.BlockSpec((None, 1, st2), blk),
        ],
        out_specs=pl.BlockSpec((gs, k, n), blk),
        out_shape=jax.ShapeDtypeStruct((g, k, n), BF16),
        compiler_params=_params("parallel"),
        name="s5_conv",
    )(at, wt, krep, vre, vim, ar, ai)


def _s5_weights(lam_re, lam_im, log_dt, b_re, b_im, c_re, c_im, d_skip):
    t = S5_CHUNK
    lr = jnp.minimum(lam_re.astype(F32), -1e-4)
    li = lam_im.astype(F32)
    dt = jnp.exp(log_dt.astype(F32))[:, None]
    mag = jnp.exp(lr * dt)
    a_re = mag * jnp.cos(li * dt)
    a_im = mag * jnp.sin(li * dt)
    den = lr * lr + li * li
    nr = a_re - 1.0
    f_re = (nr * lr + a_im * li) / den
    f_im = (a_im * lr - nr * li) / den
    br = b_re.astype(F32)
    bi = b_im.astype(F32)
    bb_re = f_re[..., None] * br - f_im[..., None] * bi
    bb_im = f_re[..., None] * bi + f_im[..., None] * br
    cr = c_re.astype(F32)
    ci = c_im.astype(F32)
    n_groups = cr.shape[0]

    def cpow(n):
        n = n.astype(F32)[:, None, None]
        m = jnp.exp(lr * dt * n)
        th = li * dt * n
        return m * jnp.cos(th), m * jnp.sin(th)

    pr, pi = cpow(jnp.arange(t + 1))
    prg, pig = pr.transpose(1, 0, 2)[:, :, None, :], pi.transpose(1, 0, 2)[:, :, None, :]
    cc = (jnp.concatenate([cr, cr], axis=-1)[:, None] * jnp.concatenate([prg, -pig], axis=-1)
          - jnp.concatenate([ci, ci], axis=-1)[:, None] * jnp.concatenate([pig, prg], axis=-1))
    ca = cc[:, :t].reshape(n_groups, t * GROUP_SIZE, 2 * STATE)
    bb = jnp.concatenate([bb_re, bb_im], axis=1)
    kst = jnp.einsum('gmp,gpk->gmk', ca, bb, precision=lax.Precision.HIGHEST)
    skip = d_skip.astype(F32).reshape(n_groups, GROUP_SIZE)
    kst = kst.at[:, :GROUP_SIZE, :].add(skip[:, :, None] * jnp.eye(GROUP_SIZE, dtype=F32))
    kst = jnp.pad(kst, ((0, 0), (0, 0), (0, 2 * STATE - GROUP_SIZE)))
    qr, qi = pr[:t][::-1], pi[:t][::-1]
    w_re = jnp.einsum('sgp,gpk->gpsk', qr, bb_re) - jnp.einsum('sgp,gpk->gpsk', qi, bb_im)
    w_im = jnp.einsum('sgp,gpk->gpsk', qr, bb_im) + jnp.einsum('sgp,gpk->gpsk', qi, bb_re)
    wt = jnp.concatenate([w_re, w_im], axis=1).reshape(n_groups, 2 * STATE, t * GROUP_SIZE)
    v = cc[:, 1:].reshape(n_groups, t * GROUP_SIZE, 2 * STATE)
    vs = jnp.roll(v, STATE, axis=-1)
    first = (jnp.arange(n_groups) % S5_PAIR == 0)[:, None, None]
    lo = (jnp.arange(2 * STATE) < STATE)[None, None, :]
    vre = jnp.where(first, jnp.where(lo, v, 0.0), jnp.where(lo, 0.0, vs)).astype(BF16)
    vim = jnp.where(first, jnp.where(lo, vs, 0.0), jnp.where(lo, 0.0, v)).astype(BF16)
    a_t = lambda x: x[t].reshape(n_groups // S5_PAIR, 1, S5_PAIR * STATE)
    return wt.astype(BF16), kst.astype(BF16), vre, vim, a_t(pr), a_t(pi)


def _tail(x, m, gpost, p, wpg, wpp):
    x1 = x + _rms(m, gpost)
    pg = _bdot(x1, wpg)
    pp = _bdot(p, wpp)
    return x1 + jax.nn.sigmoid(pg) * pp


def _s5_post_kernel(x_hbm, p_hbm, yt_ref, gate_ref, wglu_ref, bglu_ref, wout_ref, gpost_ref, wpg_ref, wpp_ref,
                    o_hbm, xbuf, pbuf, obuf, sem_x, sem_p, sem_o, *, n_steps, layer):
    k = pl.program_id(0)
    slot = _fetch_phase_rows(x_hbm, xbuf, sem_x, k, n_steps)
    _fetch_phase_rows(p_hbm, pbuf, sem_p, k, n_steps, lead=(layer,))
    n_groups = yt_ref.shape[0]
    yt = jnp.concatenate([yt_ref[g] for g in range(n_groups)], axis=0).astype(F32)
    g = jax.nn.gelu(yt).T
    z = _bdot(g, wglu_ref[...]) + bglu_ref[...]
    y = g * jax.nn.sigmoid(z)
    gate = gate_ref[...].astype(F32)
    y = y * (gate * jax.nn.sigmoid(gate))
    m = _bdot(y, wout_ref[...])
    res = _tail(xbuf[slot], m, gpost_ref[...], pbuf[slot], wpg_ref[...], wpp_ref[...])

    def out_copy(step, slot_):
        return pltpu.make_async_copy(obuf.at[slot_], _phase_rows(o_hbm, step), sem_o.at[slot_])

    @pl.when(k >= 2)
    def _():
        out_copy(k - 2, slot).wait()

    obuf[slot] = res
    out_copy(k, slot).start()

    @pl.when(k == n_steps - 1)
    def _():
        if n_steps >= 2:
            out_copy(k - 1, 1 - slot).wait()
        out_copy(k, slot).wait()


def _s5_post(x4, yt, gate, p5, layer, wglu, bglu, wout, gpost, wpg, wpp):
    bsz, nc, _, d = x4.shape
    pd = p5.shape[-1]
    n_groups = d // GROUP_SIZE
    n_steps = bsz * S5_CHUNK
    const = lambda k: (0, 0)
    return pl.pallas_call(
        functools.partial(_s5_post_kernel, n_steps=n_steps, layer=layer),
        grid=(n_steps,),
        in_specs=[
            pl.BlockSpec(memory_space=pl.ANY),
            pl.BlockSpec(memory_space=pl.ANY),
            pl.BlockSpec((n_groups, GROUP_SIZE, nc), lambda k: (0, k % S5_CHUNK, k // S5_CHUNK)),
            pl.BlockSpec((nc, d), lambda k: (k, 0)),
            pl.BlockSpec((d, d), const),
            pl.BlockSpec((1, d), const),
            pl.BlockSpec((d, d), const),
            pl.BlockSpec((1, d), const),
            pl.BlockSpec((d, d), const),
            pl.BlockSpec((pd, d), const),
        ],
        out_specs=pl.BlockSpec(memory_space=pl.ANY),
        out_shape=jax.ShapeDtypeStruct((bsz, nc, S5_CHUNK, d), F32),
        scratch_shapes=[pltpu.VMEM((2, nc, d), F32), pltpu.VMEM((2, nc, pd), F32), pltpu.VMEM((2, nc, d), F32),
                        pltpu.SemaphoreType.DMA((2,)), pltpu.SemaphoreType.DMA((2,)), pltpu.SemaphoreType.DMA((2,))],
        compiler_params=_params("arbitrary"),
        name="s5_post",
    )(x4, p5, yt, gate, wglu, bglu.reshape(1, d), wout, gpost.reshape(1, d), wpg, wpp)


def _attn_post_kernel(x_ref, og_ref, p_ref, wout_ref, gpost_ref, wpg_ref, wpp_ref, o_ref):
    m = jnp.dot(og_ref[...], wout_ref[...], preferred_element_type=F32)
    o_ref[...] = _tail(x_ref[...], m, gpost_ref[...], p_ref[...], wpg_ref[...], wpp_ref[...])


def _attn_post(x, og, p_all, layer, wout, gpost, wpg, wpp, *, tm):
    m, d = x.shape
    pd = p_all.shape[2]
    row = lambda i: (i, 0)
    const = lambda i: (0, 0)
    return pl.pallas_call(
        _attn_post_kernel,
        grid=(m // tm,),
        in_specs=[
            pl.BlockSpec((tm, d), row),
            pl.BlockSpec((tm, d), row),
            pl.BlockSpec((None, tm, pd), lambda i: (layer, i, 0)),
            pl.BlockSpec((d, d), const),
            pl.BlockSpec((1, d), const),
            pl.BlockSpec((d, d), const),
            pl.BlockSpec((pd, d), const),
        ],
        out_specs=pl.BlockSpec((tm, d), row),
        out_shape=jax.ShapeDtypeStruct((m, d), F32),
        compiler_params=_params("parallel"),
        name="attn_post",
    )(x, og, p_all, wout, gpost.reshape(1, d), wpg, wpp)


ATTN_HEADS = 4
ATTN_TQ = 128
ATTN_TQ_OUTER = 4096
LOG_F32_UNDERFLOW = 104.0
ATTN_PAIR = 2
NO_WINDOW = 1e30


def _sb_attn_kernel(q_ref, gate_ref, k_ref, v_ref, o_ref, acc_ref, c_ref):
    tq, nh = ATTN_TQ, ATTN_HEADS
    lanes = nh * HEAD_DIM
    rows = nh * tq
    n_inner = q_ref.shape[0] // tq
    io = pl.program_id(2)
    lane_head = lax.broadcasted_iota(jnp.int32, (tq, lanes), 1) // HEAD_DIM
    kr = lax.broadcasted_iota(jnp.int32, (tq, 2 * tq), 0)
    kc = lax.broadcasted_iota(jnp.int32, (tq, 2 * tq), 1)
    tri_ones = jnp.where((kr > kc) | (kc >= tq), 1.0, 0.0).astype(BF16)
    tw = 2 * tq
    tri_w = jnp.where(lax.broadcasted_iota(jnp.int32, (tw, tw), 0) > lax.broadcasted_iota(jnp.int32, (tw, tw), 1),
                      1.0, 0.0).astype(BF16)
    ones_w = jnp.ones((tw, tq), BF16)
    qrow =lax.broadcasted_iota(jnp.int32, (rows, tq), 0) % tq
    kcol = lax.broadcasted_iota(jnp.int32, (rows, tq), 1)
    causal = kcol < qrow
    scale = jnp.asarray(HEAD_DIM ** -0.5, BF16)

    def tile(q4, j, c_prev, mask):
        st = pl.multiple_of(j * tq, tq)
        kj = k_ref[pl.ds(st, tq), :]
        vj = v_ref[pl.ds(st, tq), :]
        z = lax.dot_general(q4, kj, (((1,), (1,)), ((), ())), preferred_element_type=F32)
        sp = jnp.maximum(z, 0.0) + jnp.log(1.0 + jnp.exp(-jnp.abs(z)))
        if mask is not None:
            sp = jnp.where(mask, sp, 0.0)
        both = jnp.dot(sp.astype(BF16), tri_ones, preferred_element_type=F32)
        tail, rs = both[:, :tq], both[:, tq:]
        logw = (z - sp) - tail
        if c_prev is not None:
            logw = logw - c_prev
        w = jnp.exp(logw)
        if mask is not None:
            w = jnp.where(mask, w, 0.0)
        pv = jnp.dot(w.astype(BF16), vj, preferred_element_type=F32)
        return pv, rs

    def window(q4, st, c_prev):
        kw = k_ref[pl.ds(st, tw), :]
        vw = v_ref[pl.ds(st, tw), :]
        z = lax.dot_general(q4, kw, (((1,), (1,)), ((), ())), preferred_element_type=F32)
        sp = jnp.maximum(z, 0.0) + jnp.log(1.0 + jnp.exp(-jnp.abs(z)))
        spb = sp.astype(BF16)
        tail = jnp.dot(spb, tri_w, preferred_element_type=F32)
        rs = jnp.dot(spb, ones_w, preferred_element_type=F32)
        w = jnp.exp((z - sp) - tail - jnp.concatenate([c_prev, c_prev], axis=1))
        pv = jnp.dot(w.astype(BF16), vw, preferred_element_type=F32)
        return pv, rs

    def q_pair(qp, _):
        i0 = io * n_inner + ATTN_PAIR * qp
        n_windows = i0 // 2
        has_window = n_windows > 0
        q4s, mins = [], []
        for s in range(ATTN_PAIR):
            r0 = pl.multiple_of((ATTN_PAIR * qp + s) * tq, tq)
            q = q_ref[pl.ds(r0, tq), :] * scale
            q4s.append(jnp.concatenate([jnp.where(lane_head == h, q, jnp.zeros_like(q)) for h in range(nh)], axis=0))
        for s in range(ATTN_PAIR):
            pv_d, rs_d = tile(q4s[s], i0 + s, None, causal)
            st = pl.multiple_of(jnp.maximum((i0 + s) * tq - tw, 0), tq)
            pv_w, rs_w = window(q4s[s], st, jnp.where(has_window, rs_d, NO_WINDOW))
            c = rs_d + jnp.where(has_window, rs_w, 0.0)
            acc_ref[s] = pv_d + pv_w
            c_ref[s] = c
            mins.append(jnp.min(c))

        def cond(state):
            n = state[0]
            unfinished = functools.reduce(jnp.logical_or, [m < LOG_F32_UNDERFLOW for m in state[1:]])
            return (n < n_windows) & unfinished

        def body(state):
            n = state[0]
            new_mins = []
            for s in range(ATTN_PAIR):
                st = pl.multiple_of((i0 + s) * tq - (n + 1) * tw, tq)
                c_prev = c_ref[s]
                pv, rs = window(q4s[s], st, c_prev)
                acc_ref[s] += pv
                c_new = c_prev + rs
                c_ref[s] = c_new
                new_mins.append(jnp.min(c_new))
            return (n + 1, *new_mins)

        final = lax.while_loop(cond, body, (jnp.int32(1), *mins))

        @pl.when((final[0] >= n_windows) & (final[2] < LOG_F32_UNDERFLOW))
        def _():
            pv, _ = tile(q4s[1], 0, c_ref[1], None)
            acc_ref[1] += pv

        for s in range(ATTN_PAIR):
            r0 = pl.multiple_of((ATTN_PAIR * qp + s) * tq, tq)
            o = jnp.zeros((tq, lanes), F32)
            for h in range(nh):
                o = jnp.where(lane_head == h, acc_ref[s, h * tq:(h + 1) * tq, :], o)
            gate = gate_ref[pl.ds(r0, tq), :].astype(F32)
            o_ref[pl.ds(r0, tq), :] = (o * (gate * jax.nn.sigmoid(gate))).astype(o_ref.dtype)
        return 0

    lax.fori_loop(0, n_inner // ATTN_PAIR, q_pair, 0)


def _sb_attn(qg, kv):
    b, l, d2 = qg.shape
    d = d2 // 2
    lanes = ATTN_HEADS * HEAD_DIM
    nhg = d // lanes
    tqo = min(ATTN_TQ_OUTER, l)
    return pl.pallas_call(
        _sb_attn_kernel,
        grid=(b, nhg, l // tqo),
        in_specs=[
            pl.BlockSpec((None, tqo, lanes), lambda bi, hg, i: (bi, i, hg)),
            pl.BlockSpec((None, tqo, lanes), lambda bi, hg, i: (bi, i, nhg + hg)),
            pl.BlockSpec((None, l, lanes), lambda bi, hg, i: (bi, 0, hg)),
            pl.BlockSpec((None, l, lanes), lambda bi, hg, i: (bi, 0, nhg + hg)),
        ],
        out_specs=pl.BlockSpec((None, tqo, lanes), lambda bi, hg, i: (bi, i, hg)),
        out_shape=jax.ShapeDtypeStruct((b, l, d), BF16),
        scratch_shapes=[pltpu.VMEM((ATTN_PAIR, ATTN_HEADS * ATTN_TQ, lanes), F32),
                        pltpu.VMEM((ATTN_PAIR, ATTN_HEADS * ATTN_TQ, ATTN_TQ), F32)],
        compiler_params=_params("parallel", "parallel", "arbitrary"),
        name="sb_attn",
    )(qg, qg, kv, kv)


def kernel(x, p, a_norm_pre, a_norm_post, a_w_in, a_lam_re, a_lam_im, a_log_dt, a_b_re, a_b_im, a_c_re, a_c_im, a_d_skip, a_w_glu, a_b_glu, a_w_out, kv_norm, w_kv, b_norm_pre, b_norm_post, b_w_in, b_w_out, ple_w_proj, ple_w_gate):
    bsz, seq, d = x.shape
    depth = p.shape[0]
    n_a = a_w_in.shape[0]
    tokens = bsz * seq
    nc = seq // S5_CHUNK
    xf = x.reshape(tokens, d)
    p_all = p.reshape(depth, tokens, -1)
    k_v = None
    for i in range(depth):
        wpg = ple_w_gate[i].astype(BF16)
        wpp = ple_w_proj[i].astype(BF16)
        if i < n_a:
            j = i
            x4 = xf.reshape(bsz, nc, S5_CHUNK, d)
            p5 = p.reshape(depth, bsz, nc, S5_CHUNK, -1)
            wut = a_w_in[j][:, :d].T.astype(BF16)
            wg = a_w_in[j][:, d:].astype(BF16)
            ut, gate = _s5_in_proj(x4, a_norm_pre[j], wut, wg)
            s5_ops = _s5_weights(a_lam_re[j], a_lam_im[j], a_log_dt[j], a_b_re[j], a_b_im[j],
                                 a_c_re[j], a_c_im[j], a_d_skip[j])
            yt = _s5_conv(ut, *s5_ops, nc=nc)
            xf = _s5_post(x4, yt, gate, p5, i, a_w_glu[j].astype(BF16), a_b_glu[j], a_w_out[j].astype(BF16),
                          a_norm_post[j], wpg, wpp).reshape(tokens, d)
        else:
            j = i - n_a
            qg = _norm_matmul(xf, b_norm_pre[j], b_w_in[j].astype(BF16), tm=PROJ_ROWS, name="attn_in_proj")
            og = _sb_attn(qg.reshape(bsz, seq, 2 * d), k_v)
            xf = _attn_post(xf, og.reshape(tokens, d), p_all, i, b_w_out[j].astype(BF16), b_norm_post[j], wpg, wpp,
                            tm=POST_ROWS)
        if i == n_a - 1:
            k_v = _norm_matmul(xf, kv_norm, w_kv.astype(BF16), tm=PROJ_ROWS, name="kv_proj").reshape(bsz, seq, 2 * d)
    return xf.reshape(bsz, seq, d)
```
